```python
import math
import numpy as np
import jax
import jax.numpy as jnp
from jax import lax

D_MODEL = 1024
BATCH = 8
SEQ = 2048
DEPTH = 4

M_HEADS = 4
M_DK = D_MODEL // M_HEADS
M_DV = D_MODEL // M_HEADS
M_WIDTH = M_HEADS * M_DV
M_CHUNK = 64
DN_DK = 128
DN_DV = 128
DN_HEADS = D_MODEL // DN_DV
DN_WIDTH = DN_HEADS * DN_DV
DN_QKV = 2 * DN_HEADS * DN_DK + DN_WIDTH
DN_CHUNK = 64
CONV_K = 3
SC_WIDTH = D_MODEL
N_BRANCH = 3
BRANCH_W = D_MODEL
D_FF = 4 * D_MODEL
EPS = 1e-6

PROJ_SIZES = (
    M_HEADS * M_DK,
    M_HEADS * M_DK,
    M_WIDTH,
    M_WIDTH,
    4 * M_HEADS,
    DN_QKV,
    DN_WIDTH,
    4 * DN_HEADS,
    3 * SC_WIDTH,
    N_BRANCH * D_MODEL,
)
N_PROJ = sum(PROJ_SIZES)

kernel_name = "hybrid_mlstm_deltanet_shortconv_encoder"


def _split_points():
    return np.cumsum(np.array(PROJ_SIZES))[:-1].tolist()


def _rmsnorm(x, g):
    xf = x.astype(jnp.float32)
    y = xf * lax.rsqrt(jnp.mean(jnp.square(xf), axis=-1, keepdims=True) + EPS)
    return (y * g.astype(jnp.float32)).astype(x.dtype)


def _l2norm(t):
    return t * lax.rsqrt(jnp.sum(t * t, axis=-1, keepdims=True) + EPS)


def _dwconv_centred(x, w):
    k = w.shape[0]
    return lax.conv_general_dilated(
        x, w[:, None, :].astype(x.dtype), window_strides=(1,),
        padding=[(k // 2, k // 2)], dimension_numbers=("NWC", "WIO", "NWC"),
        feature_group_count=x.shape[-1])


def _to_chunks(t, chunk):
    b, s, h = t.shape[:3]
    t = t.reshape((b, s // chunk, chunk, h) + t.shape[3:])
    return jnp.moveaxis(t, (1, 3), (0, 2))


def _from_chunks(t):
    nc, b, h, l = t.shape[:4]
    t = jnp.moveaxis(t, (0, 2), (1, 3))
    return t.reshape((b, nc * l, h) + t.shape[4:])


def _mlstm_dir(q, k, v, i_pre, f_pre):
    b, s, h, dk = k.shape
    dv = v.shape[-1]
    L = M_CHUNK
    qc, kc, vc = _to_chunks(q, L), _to_chunks(k, L), _to_chunks(v, L)
    ic = _to_chunks(i_pre, L)
    lfc = _to_chunks(jax.nn.log_sigmoid(f_pre), L)
    tril = jnp.tril(jnp.ones((L, L), dtype=bool))

    def step(carry, inp):
        c_st, n_st, m_st = carry
        qq, kk, vv, ig, lf = inp
        bcum = jnp.cumsum(lf, axis=-1)
        log_d = jnp.where(tril, bcum[..., :, None] - bcum[..., None, :] + ig[..., None, :], -jnp.inf)
        m_t = jnp.maximum(bcum + m_st[..., None], jnp.max(log_d, axis=-1))
        d = jnp.exp(log_d - m_t[..., None])
        inter = jnp.exp(bcum + m_st[..., None] - m_t)
        sc = jnp.einsum("bhld,bhsd->bhls", qq, kk) * d
        num = inter[..., None] * jnp.einsum("bhld,bhde->bhle", qq, c_st) + jnp.einsum("bhls,bhse->bhle", sc, vv)
        den = inter * jnp.einsum("bhld,bhd->bhl", qq, n_st) + jnp.sum(sc, axis=-1)
        h_out = num / jnp.maximum(jnp.abs(den), jnp.exp(-m_t))[..., None]
        a = bcum[..., -1:] - bcum + ig
        m_new = jnp.maximum(bcum[..., -1] + m_st, jnp.max(a, axis=-1))
        wgt = jnp.exp(a - m_new[..., None])
        dec = jnp.exp(bcum[..., -1] + m_st - m_new)
        kw = kk * wgt[..., None]
        c_new = dec[..., None, None] * c_st + jnp.einsum("bhld,bhle->bhde", kw, vv)
        n_new = dec[..., None] * n_st + jnp.sum(kw, axis=2)
        return (c_new, n_new, m_new), h_out

    init = (jnp.zeros((b, h, dk, dv), jnp.float32), jnp.zeros((b, h, dk), jnp.float32),
            jnp.zeros((b, h), jnp.float32))
    _, hs = lax.scan(step, init, (qc, kc, vc, ic, lfc))
    return _from_chunks(hs)


def _gated_delta_dir(q, k, v, beta, g):
    b, s, h, dk = k.shape
    dv = v.shape[-1]
    L = DN_CHUNK
    qc, kc, vc = _to_chunks(q, L), _to_chunks(k, L), _to_chunks(v, L)
    bc = _to_chunks(beta, L)
    gcum = jnp.cumsum(_to_chunks(g, L), axis=-1)
    tril = jnp.tril(jnp.ones((L, L), dtype=bool))
    strict = jnp.tril(jnp.ones((L, L), dtype=bool), -1)
    decay = jnp.exp(jnp.where(tril, gcum[..., :, None] - gcum[..., None, :], -jnp.inf))
    kb = kc * bc[..., None]
    a_mat = jnp.where(strict, jnp.einsum("nbhld,nbhsd->nbhls", kb, kc) * decay, 0.0)
    system = a_mat + jnp.eye(L, dtype=a_mat.dtype)
    rhs = jnp.concatenate([kb * jnp.exp(gcum)[..., None], vc * bc[..., None]], axis=-1)
    sol = lax.linalg.triangular_solve(system, rhs, left_side=True, lower=True, unit_diagonal=True)
    w_c, u_c = sol[..., :dk], sol[..., dk:]
    attn = jnp.einsum("nbhld,nbhsd->nbhls", qc, kc) * decay
    q_dec = qc * jnp.exp(gcum)[..., None]
    k_dec = kc * jnp.exp(gcum[..., -1:] - gcum)[..., None]
    g_last = jnp.exp(gcum[..., -1])

    def step(st, inp):
        ww, uu, at, qd, kd, gl = inp
        v_new = uu - jnp.einsum("bhld,bhde->bhle", ww, st)
        o = jnp.einsum("bhld,bhde->bhle", qd, st) + jnp.einsum("bhls,bhse->bhle", at, v_new)
        st = gl[..., None, None] * st + jnp.einsum("bhld,bhle->bhde", kd, v_new)
        return st, o

    s0 = jnp.zeros((b, h, dk, dv), jnp.float32)
    _, o = lax.scan(step, s0, (w_c, u_c, attn, q_dec, k_dec, g_last))
    return _from_chunks(o)


def _flip(t):
    return jnp.flip(t, axis=1)


def _layer(x, norm_mix_g, w_in, m_gate_b, m_norm_g, dn_conv_w, dn_a_log, dn_dt_bias,
           dn_norm_g, sc_conv_w, w_branch, w_out, norm_mlp_g, w_up, w_down):
    dt = x.dtype
    f32 = jnp.float32
    b, s, _ = x.shape
    hn = _rmsnorm(x, norm_mix_g)
    proj = jnp.einsum("bsd,dn->bsn", hn, w_in)
    (m_q, m_k, m_v, m_o, m_g, dn_qkv, dn_z, dn_g, sc_bcx, merge_pre) = jnp.split(proj, _split_points(), axis=-1)

    mq = m_q.reshape(b, s, M_HEADS, M_DK).astype(f32) * (M_DK ** -0.5)
    mk = m_k.reshape(b, s, M_HEADS, M_DK).astype(f32)
    mv = m_v.reshape(b, s, M_HEADS, M_DV).astype(f32)
    mg = m_g.reshape(b, s, 4, M_HEADS).astype(f32) + m_gate_b.astype(f32)
    h_fwd = _mlstm_dir(mq, mk, mv, mg[:, :, 0], mg[:, :, 1])
    h_bwd = _mlstm_dir(_flip(mq), _flip(mk), _flip(mv), _flip(mg[:, :, 2]), _flip(mg[:, :, 3]))
    hm = h_fwd + _flip(h_bwd)
    y_m = jax.nn.sigmoid(m_o.astype(f32)) * _rmsnorm(hm, m_norm_g.reshape(M_HEADS, M_DV)).reshape(b, s, M_WIDTH)

    qkv = jax.nn.silu(_dwconv_centred(dn_qkv, dn_conv_w))
    dq, dk_, dv_ = jnp.split(qkv, [DN_HEADS * DN_DK, 2 * DN_HEADS * DN_DK], axis=-1)
    dq = _l2norm(dq.reshape(b, s, DN_HEADS, DN_DK).astype(f32)) * (DN_DK ** -0.5)
    dk_ = _l2norm(dk_.reshape(b, s, DN_HEADS, DN_DK).astype(f32))
    dv_ = dv_.reshape(b, s, DN_HEADS, DN_DV).astype(f32)
    dg = dn_g.reshape(b, s, 4, DN_HEADS).astype(f32)
    beta = jax.nn.sigmoid(dg[:, :, 0::2])
    gdec = -jnp.exp(dn_a_log.astype(f32)) * jax.nn.softplus(dg[:, :, 1::2] + dn_dt_bias.astype(f32))
    o_fwd = _gated_delta_dir(dq, dk_, dv_, beta[:, :, 0], gdec[:, :, 0])
    o_bwd = _gated_delta_dir(_flip(dq), _flip(dk_), _flip(dv_), _flip(beta[:, :, 1]), _flip(gdec[:, :, 1]))
    od = o_fwd + _flip(o_bwd)
    y_d = (_rmsnorm(od, dn_norm_g) * jax.nn.silu(dn_z.reshape(b, s, DN_HEADS, DN_DV).astype(f32))).reshape(b, s, DN_WIDTH)

    sc_b, sc_c, sc_x = jnp.split(sc_bcx, 3, axis=-1)
    y_c = sc_b * _dwconv_centred(sc_c * sc_x, sc_conv_w)

    ys = jnp.stack([y_m.astype(dt), y_d.astype(dt), y_c], axis=2)
    branch = jnp.einsum("bsnw,nwd->bsnd", ys, w_branch)
    gates = jax.nn.sigmoid(merge_pre.reshape(b, s, N_BRANCH, D_MODEL))
    mixed = jnp.einsum("bsnd,bsnd->bsd", gates, branch)
    x = x + jnp.einsum("bsd,de->bse", mixed, w_out)

    h2 = _rmsnorm(x, norm_mlp_g)
    up = jnp.square(jax.nn.relu(jnp.einsum("bsd,df->bsf", h2, w_up)))
    x = x + jnp.einsum("bsf,fd->bsd", up, w_down)
    return x


def setup_inputs(seed: int = 0) -> dict:
    key = jax.random.key(seed)
    ks = jax.random.split(key, 16)
    f32 = jnp.float32

    def nrm(k, shape, scale):
        return jax.random.normal(k, shape, f32) * scale

    x = nrm(ks[0], (BATCH, SEQ, D_MODEL), 1.0)
    norm_mix_g = 1.0 + nrm(ks[1], (DEPTH, D_MODEL), 0.02)
    w_in = nrm(ks[2], (DEPTH, D_MODEL, N_PROJ), D_MODEL ** -0.5)
    f_base = jnp.zeros((4, M_HEADS), f32).at[1::2].set(jnp.linspace(3.0, 6.0, M_HEADS, dtype=f32))
    m_gate_b = f_base[None] + nrm(ks[3], (DEPTH, 4, M_HEADS), 0.1)
    m_norm_g = 1.0 + nrm(ks[4], (DEPTH, M_WIDTH), 0.02)
    dn_conv_w = nrm(ks[5], (DEPTH, CONV_K, DN_QKV), CONV_K ** -0.5)
    dn_a_log = jnp.log(jax.random.uniform(ks[6], (DEPTH, 2, DN_HEADS), f32, 1.0, 16.0))
    dt0 = jnp.exp(jax.random.uniform(ks[7], (DEPTH, 2, DN_HEADS), f32, math.log(1e-3), math.log(1e-1)))
    dn_dt_bias = dt0 + jnp.log(-jnp.expm1(-dt0))
    dn_norm_g = 1.0 + nrm(ks[8], (DEPTH, DN_DV), 0.02)
    sc_conv_w = nrm(ks[9], (DEPTH, CONV_K, SC_WIDTH), CONV_K ** -0.5)
    w_branch = nrm(ks[10], (DEPTH, N_BRANCH, BRANCH_W, D_MODEL), BRANCH_W ** -0.5)
    w_out = nrm(ks[11], (DEPTH, D_MODEL, D_MODEL), D_MODEL ** -0.5)
    norm_mlp_g = 1.0 + nrm(ks[12], (DEPTH, D_MODEL), 0.02)
    w_up = nrm(ks[13], (DEPTH, D_MODEL, D_FF), D_MODEL ** -0.5)
    w_down = nrm(ks[14], (DEPTH, D_FF, D_MODEL), D_FF ** -0.5)
    norm_final_g = 1.0 + nrm(ks[15], (D_MODEL,), 0.02)
    return {"x": x, "norm_mix_g": norm_mix_g, "w_in": w_in, "m_gate_b": m_gate_b,
            "m_norm_g": m_norm_g, "dn_conv_w": dn_conv_w, "dn_a_log": dn_a_log,
            "dn_dt_bias": dn_dt_bias, "dn_norm_g": dn_norm_g, "sc_conv_w": sc_conv_w,
            "w_branch": w_branch, "w_out": w_out, "norm_mlp_g": norm_mlp_g,
            "w_up": w_up, "w_down": w_down, "norm_final_g": norm_final_g}


def reference(x, norm_mix_g, w_in, m_gate_b, m_norm_g, dn_conv_w, dn_a_log, dn_dt_bias,
              dn_norm_g, sc_conv_w, w_branch, w_out, norm_mlp_g, w_up, w_down, norm_final_g):
    for l in range(DEPTH):
        x = _layer(x, norm_mix_g[l], w_in[l], m_gate_b[l], m_norm_g[l], dn_conv_w[l],
                   dn_a_log[l], dn_dt_bias[l], dn_norm_g[l], sc_conv_w[l], w_branch[l],
                   w_out[l], norm_mlp_g[l], w_up[l], w_down[l])
    return _rmsnorm(x, norm_final_g)
```

```python
import functools

import jax
import jax.numpy as jnp
from jax import lax
from jax.experimental import pallas as pl
from jax.experimental.pallas import tpu as pltpu

F32 = jnp.float32
BF16 = jnp.bfloat16

D_MODEL = 1024
M_HEADS = 4
M_DK = 256
DN_HEADS = 8
DN_DK = 128
DN_CHUNK = 64
D_FF = 4096
EPS = 1e-6

SPAN = 256
GATE_ROWS = 128
N_BIG = 14 * D_MODEL
NEG = -1e30

VMEM_LIMIT = 56 * 1024 * 1024


def _dot(a, b):
    return jnp.dot(a, b, preferred_element_type=F32)


def _dot_nt(a, b):
    return lax.dot_general(a, b, (((1,), (1,)), ((), ())), preferred_element_type=F32)


def _dot_tn(a, b):
    return lax.dot_general(a, b, (((0,), (0,)), ((), ())), preferred_element_type=F32)


def _split3(a):
    a1 = a.astype(BF16)
    r1 = a - a1.astype(F32)
    a2 = r1.astype(BF16)
    r2 = r1 - a2.astype(F32)
    return a1, a2, r2.astype(BF16)


def _rows_times_mask(rows, mask_bf):
    p1, p2, p3 = _split3(rows)
    return _dot(p1, mask_bf) + _dot(p2, mask_bf) + _dot(p3, mask_bf)


def _rows_to_cols(rows, eye_bf):
    p1, p2, p3 = _split3(rows)
    return _dot_nt(eye_bf, p1) + _dot_nt(eye_bf, p2) + _dot_nt(eye_bf, p3)


def _softplus(x):
    return jnp.maximum(x, 0.0) + jnp.log(1.0 + jnp.exp(-jnp.abs(x)))


def _sigmoid(x):
    return 1.0 / (1.0 + jnp.exp(-x))


def _in_proj_kernel(x_ref, g_ref, w_ref, wg_ref, big_ref, gt_ref, hn_ref, *, span):
    j = pl.program_id(1)

    @pl.when(j == 0)
    def _():
        x = x_ref[...]
        y = x * lax.rsqrt(jnp.mean(x * x, axis=-1, keepdims=True) + EPS) * g_ref[...]
        hn = y.astype(BF16)
        hn_ref[...] = hn
        gt = _dot_nt(wg_ref[...], hn)
        for c in range(gt_ref.shape[0]):
            gt_ref[c] = gt[:, c * span:(c + 1) * span]

    big_ref[...] = _dot(hn_ref[...], w_ref[...]).astype(big_ref.dtype)


def _in_proj(x2, g, w_big, wg_t, *, tm, tn, span):
    m = x2.shape[0]
    n = w_big.shape[1]
    return pl.pallas_call(
        functools.partial(_in_proj_kernel, span=span),
        grid=(m // tm, n // tn),
        in_specs=[
            pl.BlockSpec((tm, D_MODEL), lambda i, j: (i, 0)),
            pl.BlockSpec((1, D_MODEL), lambda i, j: (0, 0)),
            pl.BlockSpec((D_MODEL, tn), lambda i, j: (0, j)),
            pl.BlockSpec((GATE_ROWS, D_MODEL), lambda i, j: (0, 0)),
        ],
        out_specs=[
            pl.BlockSpec((tm, tn), lambda i, j: (i, j)),
            pl.BlockSpec((tm // span, GATE_ROWS, span), lambda i, j: (i, 0, 0)),
        ],
        out_shape=[
            jax.ShapeDtypeStruct((m, n), BF16),
            jax.ShapeDtypeStruct((m // span, GATE_ROWS, span), F32),
        ],
        scratch_shapes=[pltpu.VMEM((tm, D_MODEL), BF16)],
        compiler_params=pltpu.CompilerParams(
            dimension_semantics=("arbitrary", "arbitrary"), vmem_limit_bytes=VMEM_LIMIT),
        name="in_proj",
    )(x2, g, w_big, wg_t)


def _mlstm_kernel(q_ref, k_ref, v_ref, o_ref, gt_ref, gb_ref, ng_ref, y_ref,
                  cf_ref, cb_ref, hm_ref, *, seq, span):
    nc = seq // span
    scale = M_DK ** -0.5

    cf_ref[...] = jnp.zeros_like(cf_ref)
    cb_ref[...] = jnp.zeros_like(cb_ref)
    hm_ref[...] = jnp.zeros_like(hm_ref)

    def chunk_dir(c, c_ref, n_st, m_st, fwd):
        ri = lax.broadcasted_iota(jnp.int32, (span, span), 0)
        ci = lax.broadcasted_iota(jnp.int32, (span, span), 1)
        eye_bf = (ri == ci).astype(BF16)
        if fwd:
            mask = ci <= ri
            cum_bf = (ri <= ci).astype(BF16)
        else:
            mask = ci >= ri
            cum_bf = (ri >= ci).astype(BF16)
        r0 = pl.multiple_of(c * span, span)
        rows = pl.ds(r0, span)
        g8 = gt_ref[c] + gb_ref[...]
        ig_row = g8[0:1] if fwd else g8[2:3]
        f_row = g8[1:2] if fwd else g8[3:4]
        lf_row = -_softplus(-f_row)
        zero6 = jnp.zeros((6, span), F32)
        b_row = _rows_times_mask(jnp.concatenate([lf_row, lf_row, zero6], axis=0), cum_bf)[0:1]
        cols = _rows_to_cols(jnp.concatenate([b_row, ig_row, zero6], axis=0), eye_bf)
        b_col = cols[:, 0:1]
        ig_col = cols[:, 1:2]
        b_last = jnp.sum(lf_row, axis=1, keepdims=True)

        q = (q_ref[rows, :].astype(F32) * scale).astype(BF16)
        k = k_ref[rows, :]
        v = v_ref[rows, :]

        log_d = jnp.where(mask, b_col - b_row + ig_row, NEG)
        m_t = jnp.maximum(b_col + m_st, jnp.max(log_d, axis=1, keepdims=True))
        d = jnp.exp(log_d - m_t)
        inter = jnp.exp(b_col + m_st - m_t)
        sc = _dot_nt(q, k) * d
        c_st = c_ref[...]
        num = inter * _dot(q, c_st.astype(BF16)) + _dot(sc.astype(BF16), v)
        den = (inter * jnp.sum(q.astype(F32) * n_st, axis=1, keepdims=True)
               + jnp.sum(sc, axis=1, keepdims=True))
        inv = 1.0 / jnp.maximum(jnp.abs(den), jnp.exp(-m_t))
        hm_ref[rows, :] += num * inv

        a_col = b_last - b_col + ig_col
        m_new = jnp.maximum(b_last + m_st, jnp.max(a_col, axis=0, keepdims=True))
        wgt = jnp.exp(a_col - m_new)
        dec = jnp.exp(b_last + m_st - m_new)
        kw = k.astype(F32) * wgt
        c_ref[...] = dec * c_st + _dot_tn(kw.astype(BF16), v)
        n_new = dec * n_st + jnp.sum(kw, axis=0, keepdims=True)
        return n_new, m_new

    def body(c, carry):
        nf, mf, nb, mb = carry
        nf, mf = chunk_dir(c, cf_ref, nf, mf, True)
        nb, mb = chunk_dir(nc - 1 - c, cb_ref, nb, mb, False)
        return nf, mf, nb, mb

    n0 = jnp.zeros((1, M_DK), F32)
    m0 = jnp.zeros((1, 1), F32)
    lax.fori_loop(0, nc, body, (n0, m0, n0, m0))

    def finish(t, carry):
        rows = pl.ds(pl.multiple_of(t * span, span), span)
        h = hm_ref[rows, :]
        y = h * lax.rsqrt(jnp.mean(h * h, axis=-1, keepdims=True) + EPS) * ng_ref[...]
        y_ref[rows, :] = (_sigmoid(o_ref[rows, :].astype(F32)) * y).astype(y_ref.dtype)
        return carry

    lax.fori_loop(0, nc, finish, 0)


def _mlstm(big3, gt5, gate_b, norm_g, *, span):
    b, s, _ = big3.shape
    nc = s // span
    hb = M_DK
    q_spec = lambda off: pl.BlockSpec((None, s, hb), lambda i, h, off=off: (i, 0, off + h))
    return pl.pallas_call(
        functools.partial(_mlstm_kernel, seq=s, span=span),
        grid=(b, M_HEADS),
        in_specs=[
            q_spec(0), q_spec(M_HEADS), q_spec(2 * M_HEADS), q_spec(3 * M_HEADS),
            pl.BlockSpec((None, nc, None, 8, span), lambda i, h: (i, 0, h, 0, 0)),
            pl.BlockSpec((None, 8, span), lambda i, h: (h, 0, 0)),
            pl.BlockSpec((None, 1, hb), lambda i, h: (h, 0, 0)),
        ],
        out_specs=pl.BlockSpec((None, s, hb), lambda i, h: (i, 0, h)),
        out_shape=jax.ShapeDtypeStruct((b, s, D_MODEL), BF16),
        scratch_shapes=[
            pltpu.VMEM((M_DK, M_DK), F32),
            pltpu.VMEM((M_DK, M_DK), F32),
            pltpu.VMEM((s, M_DK), F32),
        ],
        compiler_params=pltpu.CompilerParams(
            dimension_semantics=("arbitrary", "arbitrary"), vmem_limit_bytes=VMEM_LIMIT),
        name="mlstm",
    )(big3, big3, big3, big3, gt5, gate_b, norm_g)


def _conv3_silu(x_ref, w_ref, seq):
    x = x_ref[...].astype(F32)
    rid = lax.broadcasted_iota(jnp.int32, x.shape, 0)
    xp = jnp.where(rid == 0, 0.0, pltpu.roll(x, 1, 0))
    xn = jnp.where(rid == seq - 1, 0.0, pltpu.roll(x, seq - 1, 0))
    y = w_ref[0:1, :] * xp + w_ref[1:2, :] * x + w_ref[2:3, :] * xn
    return y * _sigmoid(y)


def _block_inverse(a, same16, same32):
    n = a.shape[0]
    ri = lax.broadcasted_iota(jnp.int32, (n, n), 0)
    ci = lax.broadcasted_iota(jnp.int32, (n, n), 1)
    eye = (ri == ci).astype(F32)
    n1 = jnp.where(same16, a, 0.0).astype(BF16)
    x = eye - n1.astype(F32)
    n2 = _dot(n1, n1).astype(BF16)
    x = x + _dot(x.astype(BF16), n2)
    n4 = _dot(n2, n2).astype(BF16)
    x = x + _dot(x.astype(BF16), n4)
    n8 = _dot(n4, n4).astype(BF16)
    x = x + _dot(x.astype(BF16), n8)
    o1 = jnp.where(jnp.logical_and(same32, jnp.logical_not(same16)), a, 0.0).astype(BF16)
    xb = x.astype(BF16)
    x = x - _dot(_dot(xb, o1).astype(BF16), xb)
    o2 = jnp.where(same32, 0.0, a).astype(BF16)
    xb = x.astype(BF16)
    x = x - _dot(_dot(xb, o2).astype(BF16), xb)
    return x


def _deltanet_kernel(q_ref, k_ref, v_ref, z_ref, cwq_ref, cwk_ref, cwv_ref, gt_ref, p_ref,
                     ng_ref, y_ref, qn_ref, kn_ref, vn_ref,
                     wf_ref, uf_ref, qdf_ref, kdf_ref, atf_ref, glf_ref,
                     wb_ref, ub_ref, qdb_ref, kdb_ref, atb_ref, glb_ref,
                     acc_ref, *, seq, span):
    nsp = seq // span
    lc = DN_CHUNK
    nck = seq // lc
    per = span // lc

    q = _conv3_silu(q_ref, cwq_ref, seq)
    qn_ref[...] = q * lax.rsqrt(jnp.sum(q * q, axis=-1, keepdims=True) + EPS) * (DN_DK ** -0.5)
    k = _conv3_silu(k_ref, cwk_ref, seq)
    kn_ref[...] = k * lax.rsqrt(jnp.sum(k * k, axis=-1, keepdims=True) + EPS)
    vn_ref[...] = _conv3_silu(v_ref, cwv_ref, seq)
    acc_ref[...] = jnp.zeros_like(acc_ref)

    def prep(s, carry):
        rows = pl.ds(pl.multiple_of(s * span, span), span)
        ri = lax.broadcasted_iota(jnp.int32, (span, span), 0)
        ci = lax.broadcasted_iota(jnp.int32, (span, span), 1)
        same64 = (ri >> 6) == (ci >> 6)
        same32 = (ri >> 5) == (ci >> 5)
        same16 = (ri >> 4) == (ci >> 4)
        eye_bf = (ri == ci).astype(BF16)
        same_bf = same64.astype(BF16)
        kf = kn_ref[rows, :]
        qf = qn_ref[rows, :]
        vf = vn_ref[rows, :]
        k_bf = kf.astype(BF16)
        kk = _dot_nt(k_bf, k_bf)
        qk = _dot_nt(qf.astype(BF16), k_bf)
        g8 = gt_ref[s]
        zero7 = jnp.zeros((7, span), F32)
        zero5 = jnp.zeros((5, span), F32)
        for fwd, (w_ref, u_ref, qd_ref, kd_ref, at_ref, gl_ref) in (
                (True, (wf_ref, uf_ref, qdf_ref, kdf_ref, atf_ref, glf_ref)),
                (False, (wb_ref, ub_ref, qdb_ref, kdb_ref, atb_ref, glb_ref))):
            if fwd:
                beta_row = _sigmoid(g8[0:1])
                g_row = -jnp.exp(p_ref[0:1, :]) * _softplus(g8[1:2] + p_ref[2:3, :])
                tri = ci <= ri
                strict = ci < ri
                cum_bf = jnp.logical_and(same64, ri <= ci).astype(BF16)
            else:
                beta_row = _sigmoid(g8[2:3])
                g_row = -jnp.exp(p_ref[1:2, :]) * _softplus(g8[3:4] + p_ref[3:4, :])
                tri = ci >= ri
                strict = ci > ri
                cum_bf = jnp.logical_and(same64, ri >= ci).astype(BF16)
            g_rows = jnp.concatenate([g_row, zero7], axis=0)
            gc_row = _rows_times_mask(g_rows, cum_bf)[0:1]
            tot_row = _rows_times_mask(g_rows, same_bf)[0:1]
            cols = _rows_to_cols(jnp.concatenate([beta_row, gc_row, tot_row, zero5], axis=0), eye_bf)
            beta_col = cols[:, 0:1]
            gc_col = cols[:, 1:2]
            tot_col = cols[:, 2:3]
            decay = jnp.where(jnp.logical_and(same64, tri), jnp.exp(gc_col - gc_row), 0.0)
            attn = qk * decay
            a = jnp.where(strict, kk * decay * beta_col, 0.0)
            t = _block_inverse(a, same16, same32)
            e_gc = jnp.exp(gc_col)
            rhs = jnp.concatenate([(kf * (beta_col * e_gc)).astype(BF16),
                                   (vf * beta_col).astype(BF16)], axis=1)
            wu = _dot(t.astype(BF16), rhs)
            w_ref[rows, :] = wu[:, :DN_DK].astype(w_ref.dtype)
            u_ref[rows, :] = wu[:, DN_DK:]
            qd_ref[rows, :] = (qf * e_gc).astype(qd_ref.dtype)
            kd_ref[rows, :] = (kf * jnp.exp(tot_col - gc_col)).astype(kd_ref.dtype)
            gl_ref[rows, :] = jnp.exp(tot_col)
            at_ref[rows, :] = jnp.concatenate(
                [attn[i * lc:(i + 1) * lc, i * lc:(i + 1) * lc] for i in range(per)],
                axis=0).astype(at_ref.dtype)
        return carry

    lax.fori_loop(0, nsp, prep, 0)

    def step(c, st, w_ref, u_ref, qd_ref, kd_ref, at_ref, gl_ref):
        rows = pl.ds(pl.multiple_of(c * lc, lc), lc)
        wq = jnp.concatenate([w_ref[rows, :], qd_ref[rows, :]], axis=0)
        ws = _dot(wq, st.astype(BF16))
        v_new = u_ref[rows, :] - ws[:lc]
        v_bf = v_new.astype(BF16)
        acc_ref[rows, :] += ws[lc:] + _dot(at_ref[rows, :], v_bf)
        gl = gl_ref[pl.ds(c * lc, 1), :]
        return gl * st + _dot_tn(kd_ref[rows, :], v_bf)

    def body(c, carry):
        sf, sb = carry
        sf = step(c, sf, wf_ref, uf_ref, qdf_ref, kdf_ref, atf_ref, glf_ref)
        sb = step(nck - 1 - c, sb, wb_ref, ub_ref, qdb_ref, kdb_ref, atb_ref, glb_ref)
        return sf, sb

    s0 = jnp.zeros((DN_DK, DN_DK), F32)
    lax.fori_loop(0, nck, body, (s0, s0))

    def finish(t, carry):
        rows = pl.ds(pl.multiple_of(t * span, span), span)
        o = acc_ref[rows, :]
        y = o * lax.rsqrt(jnp.mean(o * o, axis=-1, keepdims=True) + EPS) * ng_ref[...]
        z = z_ref[rows, :].astype(F32)
        y_ref[rows, :] = (y * (z * _sigmoid(z))).astype(y_ref.dtype)
        return carry

    lax.fori_loop(0, nsp, finish, 0)


def _deltanet(big3, conv_w, gt5, params, norm_g, *, span):
    b, s, _ = big3.shape
    nsp = s // span
    hb = DN_DK
    base = 4 * D_MODEL // hb
    q_spec = lambda off: pl.BlockSpec((None, s, hb), lambda i, h, off=off: (i, 0, base + off + h))
    cw_spec = lambda off: pl.BlockSpec((None, 8, hb), lambda i, h, off=off: (off + h, 0, 0))
    dir_scratch = [
        pltpu.VMEM((s, DN_DK), BF16),
        pltpu.VMEM((s, DN_DK), F32),
        pltpu.VMEM((s, DN_DK), BF16),
        pltpu.VMEM((s, DN_DK), BF16),
        pltpu.VMEM((s, DN_CHUNK), BF16),
        pltpu.VMEM((s, 1), F32),
    ]
    return pl.pallas_call(
        functools.partial(_deltanet_kernel, seq=s, span=span),
        grid=(b, DN_HEADS),
        in_specs=[
            q_spec(0), q_spec(DN_HEADS), q_spec(2 * DN_HEADS), q_spec(3 * DN_HEADS),
            cw_spec(0), cw_spec(DN_HEADS), cw_spec(2 * DN_HEADS),
            pl.BlockSpec((None, nsp, None, 8, span), lambda i, h: (i, 0, M_HEADS + h, 0, 0)),
            pl.BlockSpec((None, 8, span), lambda i, h: (h, 0, 0)),
            pl.BlockSpec((1, hb), lambda i, h: (0, 0)),
        ],
        out_specs=pl.BlockSpec((None, s, hb), lambda i, h: (i, 0, h)),
        out_shape=jax.ShapeDtypeStruct((b, s, D_MODEL), BF16),
        scratch_shapes=[pltpu.VMEM((s, DN_DK), F32)] * 3 + dir_scratch + dir_scratch
        + [pltpu.VMEM((s, DN_DK), F32)],
        compiler_params=pltpu.CompilerParams(
            dimension_semantics=("arbitrary", "arbitrary"), vmem_limit_bytes=VMEM_LIMIT),
        name="deltanet",
    )(big3, big3, big3, big3, conv_w, conv_w, conv_w, gt5, params, norm_g)


def _shortconv_kernel(b_ref, c_ref, x_ref, w_ref, y_ref, *, seq):
    cx = c_ref[...].astype(F32) * x_ref[...].astype(F32)
    rid = lax.broadcasted_iota(jnp.int32, cx.shape, 0)
    cp = jnp.where(rid == 0, 0.0, pltpu.roll(cx, 1, 0))
    cn = jnp.where(rid == seq - 1, 0.0, pltpu.roll(cx, seq - 1, 0))
    conv = w_ref[0:1, :] * cp + w_ref[1:2, :] * cx + w_ref[2:3, :] * cn
    y_ref[...] = (b_ref[...].astype(F32) * conv).astype(y_ref.dtype)


def _shortconv(big3, conv_w, *, tc):
    b, s, _ = big3.shape
    base = 8 * D_MODEL // tc
    per = D_MODEL // tc
    spec = lambda off: pl.BlockSpec((None, s, tc), lambda i, j, off=off: (i, 0, base + off + j))
    return pl.pallas_call(
        functools.partial(_shortconv_kernel, seq=s),
        grid=(b, per),
        in_specs=[spec(0), spec(per), spec(2 * per),
                  pl.BlockSpec((8, tc), lambda i, j: (0, j))],
        out_specs=pl.BlockSpec((None, s, tc), lambda i, j: (i, 0, j)),
        out_shape=jax.ShapeDtypeStruct((b, s, D_MODEL), BF16),
        compiler_params=pltpu.CompilerParams(
            dimension_semantics=("arbitrary", "arbitrary"), vmem_limit_bytes=VMEM_LIMIT),
        name="shortconv",
    )(big3, big3, big3, conv_w)


def _merge_kernel(x_ref, ym_ref, yd_ref, yc_ref, p0_ref, p1_ref, p2_ref, wb_ref, wo_ref, o_ref):
    mixed = _sigmoid(p0_ref[...].astype(F32)) * _dot(ym_ref[...], wb_ref[0])
    mixed += _sigmoid(p1_ref[...].astype(F32)) * _dot(yd_ref[...], wb_ref[1])
    mixed += _sigmoid(p2_ref[...].astype(F32)) * _dot(yc_ref[...], wb_ref[2])
    o_ref[...] = x_ref[...] + _dot(mixed.astype(BF16), wo_ref[...])


def _merge(x2, ym, yd, yc, big2, w_branch, w_out, *, tm):
    m = x2.shape[0]
    row = lambda: pl.BlockSpec((tm, D_MODEL), lambda i: (i, 0))
    pre = lambda n: pl.BlockSpec((tm, D_MODEL), lambda i, n=n: (i, 11 + n))
    return pl.pallas_call(
        _merge_kernel,
        grid=(m // tm,),
        in_specs=[row(), row(), row(), row(), pre(0), pre(1), pre(2),
                  pl.BlockSpec((3, D_MODEL, D_MODEL), lambda i: (0, 0, 0)),
                  pl.BlockSpec((D_MODEL, D_MODEL), lambda i: (0, 0))],
        out_specs=row(),
        out_shape=jax.ShapeDtypeStruct((m, D_MODEL), F32),
        compiler_params=pltpu.CompilerParams(
            dimension_semantics=("arbitrary",), vmem_limit_bytes=VMEM_LIMIT),
        name="merge",
    )(x2, ym, yd, yc, big2, big2, big2, w_branch, w_out)


def _mlp_kernel(x_ref, g_ref, wu_ref, wd_ref, gf_ref, o_ref, *, ff_chunk, final_norm):
    x = x_ref[...]
    h = (x * lax.rsqrt(jnp.mean(x * x, axis=-1, keepdims=True) + EPS) * g_ref[...]).astype(BF16)
    out = x
    for c in range(D_FF // ff_chunk):
        cols = slice(c * ff_chunk, (c + 1) * ff_chunk)
        up = jnp.maximum(_dot(h, wu_ref[:, cols]), 0.0)
        out = out + _dot((up * up).astype(BF16), wd_ref[cols, :])
    if final_norm:
        out = out * lax.rsqrt(jnp.mean(out * out, axis=-1, keepdims=True) + EPS) * gf_ref[...]
    o_ref[...] = out


def _mlp(x2, g, w_up, w_down, g_final, *, tm, final_norm):
    m = x2.shape[0]
    return pl.pallas_call(
        functools.partial(_mlp_kernel, ff_chunk=1024, final_norm=final_norm),
        grid=(m // tm,),
        in_specs=[
            pl.BlockSpec((tm, D_MODEL), lambda i: (i, 0)),
            pl.BlockSpec((1, D_MODEL), lambda i: (0, 0)),
            pl.BlockSpec((D_MODEL, D_FF), lambda i: (0, 0)),
            pl.BlockSpec((D_FF, D_MODEL), lambda i: (0, 0)),
            pl.BlockSpec((1, D_MODEL), lambda i: (0, 0)),
        ],
        out_specs=pl.BlockSpec((tm, D_MODEL), lambda i: (i, 0)),
        out_shape=jax.ShapeDtypeStruct((m, D_MODEL), F32),
        compiler_params=pltpu.CompilerParams(
            dimension_semantics=("arbitrary",), vmem_limit_bytes=VMEM_LIMIT),
        name="mlp",
    )(x2, g, w_up, w_down, g_final)


def _prep_w_in(w_in):
    d = w_in.shape[0]
    c_mg = 4 * D_MODEL
    c_dn = c_mg + 4 * M_HEADS
    c_dg = c_dn + 4 * D_MODEL
    c_sc = c_dg + 4 * DN_HEADS
    w_big = jnp.concatenate([w_in[:, :c_mg], w_in[:, c_dn:c_dg], w_in[:, c_sc:]], axis=1)
    mg = w_in[:, c_mg:c_dn].reshape(d, 4, M_HEADS).transpose(2, 1, 0)
    dg = w_in[:, c_dg:c_sc].reshape(d, 4, DN_HEADS).transpose(2, 1, 0)
    pad = lambda t: jnp.pad(t, ((0, 0), (0, 4), (0, 0))).reshape(-1, d)
    wg_t = jnp.concatenate([pad(mg), pad(dg)], axis=0)
    wg_t = jnp.pad(wg_t, ((0, GATE_ROWS - wg_t.shape[0]), (0, 0)))
    return w_big.astype(BF16), wg_t.astype(BF16)


def _rows8(vals, width):
    h, r = vals.shape
    t = jnp.pad(vals.astype(F32), ((0, 0), (0, 8 - r)))
    return jnp.broadcast_to(t[:, :, None], (h, 8, width))


def _conv_rows(w, width):
    k, c = w.shape
    t = w.astype(F32).reshape(k, c // width, width).transpose(1, 0, 2)
    return jnp.pad(t, ((0, 0), (0, 8 - k), (0, 0)))


def _layer(x2, batch, seq, p, g_final, final_norm):
    m = x2.shape[0]
    span = SPAN
    w_big, wg_t = _prep_w_in(p["w_in"])
    big, gt = _in_proj(x2, p["norm_mix_g"].reshape(1, -1), w_big, wg_t, tm=1024, tn=1024, span=span)
    big3 = big.reshape(batch, seq, N_BIG)
    gt5 = gt.reshape(batch, seq // span, GATE_ROWS // 8, 8, span)

    gate_b = _rows8(p["m_gate_b"].T, span)
    ym = _mlstm(big3, gt5, gate_b, p["m_norm_g"].reshape(M_HEADS, 1, M_DK), span=span)

    dn_params = _rows8(jnp.concatenate([p["dn_a_log"].T, p["dn_dt_bias"].T], axis=1), span)
    yd = _deltanet(big3, _conv_rows(p["dn_conv_w"], DN_DK), gt5, dn_params,
                   p["dn_norm_g"].reshape(1, -1), span=span)

    sc_w = jnp.pad(p["sc_conv_w"].astype(F32), ((0, 5), (0, 0)))
    yc = _shortconv(big3, sc_w, tc=256)

    x2 = _merge(x2, ym.reshape(m, -1), yd.reshape(m, -1), yc.reshape(m, -1), big,
                p["w_branch"].astype(BF16), p["w_out"].astype(BF16), tm=512)
    x2 = _mlp(x2, p["norm_mlp_g"].reshape(1, -1), p["w_up"].astype(BF16),
              p["w_down"].astype(BF16), g_final.reshape(1, -1), tm=512, final_norm=final_norm)
    return x2


def kernel(x, norm_mix_g, w_in, m_gate_b, m_norm_g, dn_conv_w, dn_a_log, dn_dt_bias,
           dn_norm_g, sc_conv_w, w_branch, w_out, norm_mlp_g, w_up, w_down, norm_final_g):
    batch, seq, d = x.shape
    depth = w_in.shape[0]
    x2 = x.reshape(batch * seq, d)
    stacked = dict(norm_mix_g=norm_mix_g, w_in=w_in, m_gate_b=m_gate_b, m_norm_g=m_norm_g,
                   dn_conv_w=dn_conv_w, dn_a_log=dn_a_log, dn_dt_bias=dn_dt_bias,
                   dn_norm_g=dn_norm_g, sc_conv_w=sc_conv_w, w_branch=w_branch, w_out=w_out,
                   norm_mlp_g=norm_mlp_g, w_up=w_up, w_down=w_down)
    for l in range(depth):
        p = {k: v[l] for k, v in stacked.items()}
        x2 = _layer(x2, batch, seq, p, norm_final_g, final_norm=(l == depth - 1))
    return x2.reshape(batch, seq, d)
```

```python
import functools

import jax
import jax.numpy as jnp
from jax import lax
from jax.experimental import pallas as pl
from jax.experimental.pallas import tpu as pltpu

F32 = jnp.float32
BF16 = jnp.bfloat16

D_MODEL = 1024
M_HEADS = 4
M_DK = 256
DN_HEADS = 8
DN_DK = 128
DN_CHUNK = 64
D_FF = 4096
EPS = 1e-6

SPAN = 256
GATE_ROWS = 128
N_BIG = 14 * D_MODEL
NEG = -1e30
HEADS_PER_STEP = 2

VMEM_LIMIT = 56 * 1024 * 1024


def _dot(a, b):
    return jnp.dot(a, b, preferred_element_type=F32)


def _dot_nt(a, b):
    return lax.dot_general(a, b, (((1,), (1,)), ((), ())), preferred_element_type=F32)


def _dot_tn(a, b):
    return lax.dot_general(a, b, (((0,), (0,)), ((), ())), preferred_element_type=F32)


def _split3_rows(a, pad_to=None):
    a1 = a.astype(BF16).astype(F32)
    r1 = a - a1
    a2 = r1.astype(BF16).astype(F32)
    a3 = r1 - a2
    parts = [a1, a2, a3]
    if pad_to is not None:
        parts.append(jnp.zeros((pad_to - 3 * a.shape[0], a.shape[1]), F32))
    return jnp.concatenate(parts, axis=0).astype(BF16)


def _scan_max(x, lane, forward):
    n = x.shape[1]
    sh = 1
    while sh < n:
        if forward:
            x = jnp.where(lane >= sh, jnp.maximum(x, pltpu.roll(x, sh, 1)), x)
        else:
            x = jnp.where(lane < n - sh, jnp.maximum(x, pltpu.roll(x, n - sh, 1)), x)
        sh *= 2
    return x


def _rows_times_mask(rows, mask_bf):
    p = _dot(_split3_rows(rows), mask_bf)
    return (p[0:8] + p[8:16]) + p[16:24]


def _rows_to_cols(rows, eye_bf):
    p = _dot_nt(eye_bf, _split3_rows(rows))
    return (p[:, 0:8] + p[:, 8:16]) + p[:, 16:24]


def _softplus(x):
    return jnp.maximum(x, 0.0) + jnp.log(1.0 + jnp.exp(-jnp.abs(x)))


def _sigmoid(x):
    return 1.0 / (1.0 + jnp.exp(-x))


def _in_proj_kernel(x_ref, g_ref, w_ref, wg_ref, big_ref, gt_ref, hn_ref, *, span):
    j = pl.program_id(1)

    @pl.when(j == 0)
    def _():
        x = x_ref[...]
        y = x * lax.rsqrt(jnp.mean(x * x, axis=-1, keepdims=True) + EPS) * g_ref[...]
        hn = y.astype(BF16)
        hn_ref[...] = hn
        gt = _dot_nt(wg_ref[...], hn)
        for c in range(gt_ref.shape[0]):
            gt_ref[c] = gt[:, c * span:(c + 1) * span]

    big_ref[...] = _dot(hn_ref[...], w_ref[...]).astype(big_ref.dtype)


def _in_proj(x2, g, w_big, wg_t, *, tm, tn, span):
    m = x2.shape[0]
    n = w_big.shape[1]
    return pl.pallas_call(
        functools.partial(_in_proj_kernel, span=span),
        grid=(m // tm, n // tn),
        in_specs=[
            pl.BlockSpec((tm, D_MODEL), lambda i, j: (i, 0)),
            pl.BlockSpec((1, D_MODEL), lambda i, j: (0, 0)),
            pl.BlockSpec((D_MODEL, tn), lambda i, j: (0, j)),
            pl.BlockSpec((GATE_ROWS, D_MODEL), lambda i, j: (0, 0)),
        ],
        out_specs=[
            pl.BlockSpec((tm, tn), lambda i, j: (i, j)),
            pl.BlockSpec((tm // span, GATE_ROWS, span), lambda i, j: (i, 0, 0)),
        ],
        out_shape=[
            jax.ShapeDtypeStruct((m, n), BF16),
            jax.ShapeDtypeStruct((m // span, GATE_ROWS, span), F32),
        ],
        scratch_shapes=[pltpu.VMEM((tm, D_MODEL), BF16)],
        compiler_params=pltpu.CompilerParams(
            dimension_semantics=("arbitrary", "arbitrary"), vmem_limit_bytes=VMEM_LIMIT),
        name="in_proj",
    )(x2, g, w_big, wg_t)


def _mlstm_kernel(q_ref, k_ref, v_ref, o_ref, gt_ref, gb_ref, ng_ref, y_ref,
                  c_ref, hm_ref, rep_ref, rrow_ref, *, seq, span, hb):
    nc = seq // span
    scale = M_DK ** -0.5
    rw = 128
    chains = [(hh, fwd) for hh in range(hb) for fwd in (True, False)]
    nch = len(chains)
    each = lambda f, *ls: [f(*a) for a in zip(*ls)]
    wide = lambda t: jnp.concatenate([t] * (span // rw), axis=1)

    c_ref[...] = jnp.zeros_like(c_ref)
    hm_ref[...] = jnp.zeros_like(hm_ref)

    def gates(c, carry):
        ri = lax.broadcasted_iota(jnp.int32, (span, span), 0)
        ci = lax.broadcasted_iota(jnp.int32, (span, span), 1)
        eye_bf = (ri == ci).astype(BF16)
        cum = {True: (ri <= ci).astype(BF16), False: (ri >= ci).astype(BF16)}
        lane = lax.broadcasted_iota(jnp.int32, (8, span), 1)
        ek = lax.broadcasted_iota(jnp.int32, (rw, 3 * rw), 0)
        el = lax.broadcasted_iota(jnp.int32, (rw, 3 * rw), 1)
        pick_bf = jnp.logical_and(ek < 24, (ek & 7) == (el >> 7)).astype(BF16)
        zero7 = jnp.zeros((7, span), F32)
        zero6 = jnp.zeros((6, span), F32)
        zero5 = jnp.zeros((5, span), F32)
        work = [(n, hh, f, c * gpi + j) for j in range(gpi) for n, (hh, f) in enumerate(chains)]
        g8 = [gt_ref[x, hh] + gb_ref[hh] for _, hh, _, x in work]
        ig = [g[0:1] if f else g[2:3] for g, (_, _, f, _) in zip(g8, work)]
        lf = [-_softplus(-(g[1:2] if f else g[3:4])) for g, (_, _, f, _) in zip(g8, work)]
        b = [_rows_times_mask(jnp.concatenate([l, zero7], axis=0), cum[f])[0:1]
             for l, (_, _, f, _) in zip(lf, work)]
        r = each(lambda i, bb: i - bb, ig, b)
        mx = [_scan_max(jnp.broadcast_to(x, (8, span)), lane, f)[0:1] for x, (_, _, f, _) in zip(r, work)]
        pieces = [_split3_rows(jnp.concatenate([bb, i, m, zero5], axis=0), pad_to=rw)
                  for bb, i, m in zip(b, ig, mx)]
        cols = [_dot_nt(eye_bf, p) for p in pieces]
        rep = [_dot(t.astype(BF16), pick_bf) for t in cols]
        for w, (n, _, _, x) in enumerate(work):
            rep_ref[n, x] = rep[w]
            rrow_ref[n, x] = jnp.concatenate([r[w], lf[w], zero6], axis=0)
        return carry

    gpi = 4 if nc % 4 == 0 else 1
    lax.fori_loop(0, nc // gpi, gates, 0)

    def body(c, carry):
        n_st = list(carry[:nch])
        m_st = list(carry[nch:])
        ri = lax.broadcasted_iota(jnp.int32, (span, span), 0)
        ci = lax.broadcasted_iota(jnp.int32, (span, span), 1)
        mask = {True: ci <= ri, False: ci >= ri}
        ones_bf = jnp.ones((span, rw), BF16)
        cc = [c if f else nc - 1 - c for _, f in chains]
        rows = [pl.ds(pl.multiple_of(x * span, span), span) for x in cc]
        hcols = [slice(hh * M_DK, (hh + 1) * M_DK) for hh, _ in chains]
        rep = [rep_ref[n, x] for n, x in enumerate(cc)]
        rr = [rrow_ref[n, x] for n, x in enumerate(cc)]
        b_rep = [t[:, 0:rw] for t in rep]
        ig_rep = [t[:, rw:2 * rw] for t in rep]
        mx_rep = [t[:, 2 * rw:3 * rw] for t in rep]
        r_row = [t[0:1] for t in rr]
        b_last = [jnp.sum(t[1:2], axis=1, keepdims=True) for t in rr]
        q = [(q_ref[rw_, cs].astype(F32) * scale).astype(BF16) for rw_, cs in zip(rows, hcols)]
        k = [k_ref[rw_, cs] for rw_, cs in zip(rows, hcols)]
        v = [v_ref[rw_, cs] for rw_, cs in zip(rows, hcols)]
        s = each(_dot_nt, q, k)
        c_st = [c_ref[n] for n in range(nch)]
        qc = each(lambda a, t: _dot(a, t.astype(BF16)), q, c_st)
        qn = each(lambda a, t: _dot_nt(a, jnp.broadcast_to(t, (rw, M_DK)).astype(BF16)), q, n_st)
        c1 = each(lambda m, t: -jnp.maximum(m, t), m_st, mx_rep)
        inter = each(lambda m, t: jnp.exp(m + t), m_st, c1)
        e_negm = each(lambda t, bb: jnp.exp(t - bb), c1, b_rep)
        d = [jnp.exp(jnp.where(mask[f], rr_ + wide(t), NEG)) for rr_, t, (_, f) in zip(r_row, c1, chains)]
        sc = each(lambda a, t: (a * t).astype(BF16), s, d)
        scv = each(_dot, sc, v)
        rs = [_dot(t, ones_bf) for t in sc]
        den = each(lambda i, a, t: i * a + t, inter, qn, rs)
        inv = each(lambda t, e: 1.0 / jnp.maximum(jnp.abs(t), e), den, e_negm)
        for n, (hh, _) in enumerate(chains):
            hm_ref[hh, rows[n], :] += wide(inter[n] * inv[n]) * qc[n] + wide(inv[n]) * scv[n]

        a_rep = each(lambda bl, bb, i: bl - bb + i, b_last, b_rep, ig_rep)
        m_new = each(lambda bl, m, a: jnp.maximum(
            bl + m, jnp.max(jnp.max(a, axis=0, keepdims=True), axis=1, keepdims=True)),
            b_last, m_st, a_rep)
        wgt = each(lambda a, m: jnp.exp(a - m), a_rep, m_new)
        dec = each(lambda bl, m, mn: jnp.exp(bl + m - mn), b_last, m_st, m_new)
        kw = each(lambda a, w: a.astype(F32) * wide(w), k, wgt)
        upd = each(lambda a, t: _dot_tn(a.astype(BF16), t), kw, v)
        for n in range(nch):
            c_ref[n] = dec[n] * c_st[n] + upd[n]
        n_new = each(lambda dd, t, a: dd * t + jnp.sum(a, axis=0, keepdims=True), dec, n_st, kw)
        return tuple(n_new) + tuple(m_new)

    n0 = jnp.zeros((1, M_DK), F32)
    m0 = jnp.zeros((1, 1), F32)
    lax.fori_loop(0, nc, body, (n0,) * nch + (m0,) * nch)

    def finish(t, carry):
        rows = pl.ds(pl.multiple_of(t * span, span), span)
        for hh in range(hb):
            cols_h = slice(hh * M_DK, (hh + 1) * M_DK)
            h = hm_ref[hh, rows, :]
            y = h * lax.rsqrt(jnp.mean(h * h, axis=-1, keepdims=True) + EPS) * ng_ref[hh]
            y_ref[rows, cols_h] = (_sigmoid(o_ref[rows, cols_h].astype(F32)) * y).astype(y_ref.dtype)
        return carry

    lax.fori_loop(0, nc, finish, 0)


def _mlstm(big3, gt5, gate_b, norm_g, *, span, hb):
    b, s, _ = big3.shape
    nc = s // span
    wb = hb * M_DK
    groups = M_HEADS // hb
    q_spec = lambda off: pl.BlockSpec((None, s, wb), lambda i, h, off=off: (i, 0, off + h))
    return pl.pallas_call(
        functools.partial(_mlstm_kernel, seq=s, span=span, hb=hb),
        grid=(b, groups),
        in_specs=[
            q_spec(0), q_spec(groups), q_spec(2 * groups), q_spec(3 * groups),
            pl.BlockSpec((None, nc, hb, 8, span), lambda i, h: (i, 0, h, 0, 0)),
            pl.BlockSpec((hb, 8, span), lambda i, h: (h, 0, 0)),
            pl.BlockSpec((hb, 1, M_DK), lambda i, h: (h, 0, 0)),
        ],
        out_specs=pl.BlockSpec((None, s, wb), lambda i, h: (i, 0, h)),
        out_shape=jax.ShapeDtypeStruct((b, s, D_MODEL), BF16),
        scratch_shapes=[
            pltpu.VMEM((2 * hb, M_DK, M_DK), F32),
            pltpu.VMEM((hb, s, M_DK), F32),
            pltpu.VMEM((2 * hb, nc, span, 3 * 128), F32),
            pltpu.VMEM((2 * hb, nc, 8, span), F32),
        ],
        compiler_params=pltpu.CompilerParams(
            dimension_semantics=("arbitrary", "arbitrary"), vmem_limit_bytes=VMEM_LIMIT),
        name="mlstm",
    )(big3, big3, big3, big3, gt5, gate_b, norm_g)


def _conv3_silu(x, w, seq):
    rid = lax.broadcasted_iota(jnp.int32, x.shape, 0)
    xp = jnp.where(rid == 0, 0.0, pltpu.roll(x, 1, 0))
    xn = jnp.where(rid == seq - 1, 0.0, pltpu.roll(x, seq - 1, 0))
    y = w[0:1, :] * xp + w[1:2, :] * x + w[2:3, :] * xn
    return y * _sigmoid(y)


def _log2(n):
    return n.bit_length() - 1


def _deltanet_kernel(q_ref, k_ref, v_ref, z_ref, cwq_ref, cwk_ref, cwv_ref, gt_ref, p_ref,
                     ng_ref, y_ref, qn_ref, kn_ref, vn_ref, pq_ref, nn_ref, gl_ref, acc_ref,
                     *, seq, span, hb):
    nsp = seq // span
    lc = DN_CHUNK
    nck = seq // lc
    per = span // lc
    sh = _log2(lc)
    spi = 2

    for hh in range(hb):
        cols_h = slice(hh * DN_DK, (hh + 1) * DN_DK)
        q = _conv3_silu(q_ref[:, cols_h].astype(F32), cwq_ref[hh], seq)
        qn_ref[hh] = q * lax.rsqrt(jnp.sum(q * q, axis=-1, keepdims=True) + EPS) * (DN_DK ** -0.5)
        k = _conv3_silu(k_ref[:, cols_h].astype(F32), cwk_ref[hh], seq)
        kn_ref[hh] = k * lax.rsqrt(jnp.sum(k * k, axis=-1, keepdims=True) + EPS)
        vn_ref[hh] = _conv3_silu(v_ref[:, cols_h].astype(F32), cwv_ref[hh], seq)

    def fold(full):
        lane_grp = lax.broadcasted_iota(jnp.int32, (lc, span), 1) >> sh
        out = full[(per - 1) * lc:per * lc]
        for c in range(per - 2, -1, -1):
            out = jnp.where(lane_grp == c, full[c * lc:(c + 1) * lc], out)
        return out

    def widen(col):
        lane_grp = lax.broadcasted_iota(jnp.int32, (lc, span), 1) >> sh
        out = jnp.broadcast_to(col[(per - 1) * lc:per * lc], (lc, span))
        for c in range(per - 2, -1, -1):
            out = jnp.where(lane_grp == c, jnp.broadcast_to(col[c * lc:(c + 1) * lc], (lc, span)), out)
        return out

    def blockdiag(wide_bf):
        ri = lax.broadcasted_iota(jnp.int32, (span, span), 0)
        ci = lax.broadcasted_iota(jnp.int32, (span, span), 1)
        tiled = jnp.concatenate([wide_bf] * per, axis=0)
        return jnp.where((ri >> sh) == (ci >> sh), tiled, jnp.zeros_like(tiled))

    def wide_inverse(aws, ri, cj):
        eye = (ri == cj).astype(F32)
        same16 = (ri >> 4) == (cj >> 4)
        n1 = [jnp.where(same16, aw, 0.0).astype(BF16) for aw in aws]
        xs = [eye - n.astype(F32) for n in n1]
        nb = [blockdiag(n) for n in n1]
        for _ in range(3):
            n1 = [_dot(n, b).astype(BF16) for n, b in zip(n1, nb)]
            nb = [blockdiag(n) for n in n1]
            xs = [x + _dot(x.astype(BF16), b) for x, b in zip(xs, nb)]
        lvl = 4
        while (1 << lvl) < lc:
            inner = (ri >> lvl) == (cj >> lvl)
            outer = (ri >> (lvl + 1)) == (cj >> (lvl + 1))
            pick = jnp.logical_and(outer, jnp.logical_not(inner))
            ob = [blockdiag(jnp.where(pick, aw, 0.0).astype(BF16)) for aw in aws]
            xb = [x.astype(BF16) for x in xs]
            xo = [_dot(x, o).astype(BF16) for x, o in zip(xb, ob)]
            xs = [x - _dot(t, blockdiag(b)) for x, t, b in zip(xs, xo, xb)]
            lvl += 1
        return xs

    def prep(it, carry):
        ri_f = lax.broadcasted_iota(jnp.int32, (span, span), 0)
        ci_f = lax.broadcasted_iota(jnp.int32, (span, span), 1)
        same_f = (ri_f >> sh) == (ci_f >> sh)
        eye_bf = (ri_f == ci_f).astype(BF16)
        cum_bf = jnp.concatenate([jnp.logical_and(same_f, ri_f <= ci_f).astype(BF16),
                                  jnp.logical_and(same_f, ri_f >= ci_f).astype(BF16)], axis=1)
        ri = lax.broadcasted_iota(jnp.int32, (lc, span), 0)
        cj = lax.broadcasted_iota(jnp.int32, (lc, span), 1) & (lc - 1)
        zero6 = jnp.zeros((6, span), F32)
        zero2 = jnp.zeros((2, span), F32)

        units = [(hh, it * spi + j) for j in range(spi) for hh in range(hb)]
        rows_u = [pl.ds(pl.multiple_of(s * span, span), span) for _, s in units]
        kf = [kn_ref[hh, r, :] for (hh, _), r in zip(units, rows_u)]
        qf = [qn_ref[hh, r, :] for (hh, _), r in zip(units, rows_u)]
        vf = [vn_ref[hh, r, :] for (hh, _), r in zip(units, rows_u)]
        k_bf = [k.astype(BF16) for k in kf]
        kq = [_dot_nt(jnp.concatenate([k, q.astype(BF16)], axis=0), k) for k, q in zip(k_bf, qf)]
        kkw = [fold(t[:span]) for t in kq]
        qkw = [fold(t[span:]) for t in kq]

        g8 = [gt_ref[s, hh] for hh, s in units]
        pr = [p_ref[hh] for hh, _ in units]
        beta_f = [_sigmoid(g[0:1]) for g in g8]
        beta_b = [_sigmoid(g[2:3]) for g in g8]
        g_f = [-jnp.exp(p[0:1]) * _softplus(g[1:2] + p[2:3]) for g, p in zip(g8, pr)]
        g_b = [-jnp.exp(p[1:2]) * _softplus(g[3:4] + p[3:4]) for g, p in zip(g8, pr)]
        g2 = [jnp.concatenate([a, b, zero6], axis=0) for a, b in zip(g_f, g_b)]
        cum = [_rows_times_mask(g, cum_bf) for g in g2]
        cum_fwd = [c[:, :span] for c in cum]
        cum_rev = [c[:, span:] for c in cum]
        tot = [a + b - g for a, b, g in zip(cum_fwd, cum_rev, g2)]
        cols = [_rows_to_cols(jnp.concatenate([bf, cf[0:1], t[0:1], bb, cr[1:2], t[1:2], zero2], axis=0),
                              eye_bf)
                for bf, cf, t, bb, cr in zip(beta_f, cum_fwd, tot, beta_b, cum_rev)]

        chains = [(u, fwd) for u in range(len(units)) for fwd in (True, False)]
        col = lambda u, fwd, k: cols[u][:, (0 if fwd else 3) + k:(0 if fwd else 3) + k + 1]
        beta_col = [col(u, f, 0) for u, f in chains]
        gc_col = [col(u, f, 1) for u, f in chains]
        tot_col = [col(u, f, 2) for u, f in chains]
        gc_row = [cum_fwd[u][0:1] if f else cum_rev[u][1:2] for u, f in chains]
        tot_row = [tot[u][0:1] if f else tot[u][1:2] for u, f in chains]
        decay = [jnp.where((cj <= ri) if f else (cj >= ri), jnp.exp(widen(gc) - gr), 0.0)
                 for (u, f), gc, gr in zip(chains, gc_col, gc_row)]
        attn = [qkw[u] * d for (u, f), d in zip(chains, decay)]
        aws = [jnp.where((cj < ri) if f else (cj > ri), kkw[u] * d * widen(bc), 0.0)
               for (u, f), d, bc in zip(chains, decay, beta_col)]
        tws = wide_inverse(aws, ri, cj)
        e_gc = [jnp.exp(gc) for gc in gc_col]
        rhs = [jnp.concatenate([(kf[u] * (bc * e)).astype(BF16), (vf[u] * bc).astype(BF16)], axis=1)
               for (u, f), bc, e in zip(chains, beta_col, e_gc)]
        wu_bf = [_dot(blockdiag(t.astype(BF16)), r).astype(BF16) for t, r in zip(tws, rhs)]
        awu = [_dot(blockdiag(a.astype(BF16)), w) for a, w in zip(attn, wu_bf)]
        qp = [(qf[u] * e - t[:, :DN_DK]).astype(BF16) for (u, f), e, t in zip(chains, e_gc, awu)]
        kd = [(kf[u] * jnp.exp(tc - gc)).astype(BF16) for (u, f), tc, gc in zip(chains, tot_col, gc_col)]
        gl_row = [jnp.exp(t) for t in tot_row]
        for n, (u, f) in enumerate(chains):
            hh, s = units[u]
            idx = 2 * hh + (0 if f else 1)
            for c in range(per):
                cr = slice(c * lc, (c + 1) * lc)
                pn = _dot_tn(kd[n][cr], wu_bf[n][cr])
                chunk = s * per + c
                pq_ref[idx, chunk, 0:DN_DK, :] = pn[:, :DN_DK].astype(BF16)
                pq_ref[idx, chunk, DN_DK:DN_DK + lc, :] = qp[n][cr]
                nn_ref[idx, chunk] = pn[:, DN_DK:]
                gl_ref[idx, chunk] = jnp.broadcast_to(gl_row[n][:, c * lc:c * lc + 1], (1, DN_DK))
        for u, (hh, s) in enumerate(units):
            acc_ref[hh, rows_u[u], :] = awu[2 * u][:, DN_DK:] + awu[2 * u + 1][:, DN_DK:]
        return carry

    lax.fori_loop(0, nsp // spi, prep, 0)

    def step(c, st, hh, idx):
        r = _dot(pq_ref[idx, c], st.astype(BF16))
        rows = pl.ds(pl.multiple_of(c * lc, lc), lc)
        acc_ref[hh, rows, :] += r[DN_DK:]
        return gl_ref[idx, c] * st - r[:DN_DK] + nn_ref[idx, c]

    def body(c, carry):
        out = []
        for hh in range(hb):
            out.append(step(c, carry[2 * hh], hh, 2 * hh))
            out.append(step(nck - 1 - c, carry[2 * hh + 1], hh, 2 * hh + 1))
        return tuple(out)

    s0 = jnp.zeros((DN_DK, DN_DK), F32)
    lax.fori_loop(0, nck, body, (s0,) * (2 * hb))

    def finish(t, carry):
        rows = pl.ds(pl.multiple_of(t * span, span), span)
        for hh in range(hb):
            cols_h = slice(hh * DN_DK, (hh + 1) * DN_DK)
            o = acc_ref[hh, rows, :]
            y = o * lax.rsqrt(jnp.mean(o * o, axis=-1, keepdims=True) + EPS) * ng_ref[...]
            z = z_ref[rows, cols_h].astype(F32)
            y_ref[rows, cols_h] = (y * (z * _sigmoid(z))).astype(y_ref.dtype)
        return carry

    lax.fori_loop(0, nsp, finish, 0)


def _deltanet(big3, conv_w, gt5, params, norm_g, *, span, hb):
    b, s, _ = big3.shape
    nsp = s // span
    nck = s // DN_CHUNK
    wb = hb * DN_DK
    groups = DN_HEADS // hb
    base = 4 * D_MODEL // wb
    slot0 = M_HEADS // hb
    q_spec = lambda off: pl.BlockSpec((None, s, wb), lambda i, h, off=off: (i, 0, base + off + h))
    cw_spec = lambda off: pl.BlockSpec((hb, 8, DN_DK), lambda i, h, off=off: (off + h, 0, 0))
    return pl.pallas_call(
        functools.partial(_deltanet_kernel, seq=s, span=span, hb=hb),
        grid=(b, groups),
        in_specs=[
            q_spec(0), q_spec(groups), q_spec(2 * groups), q_spec(3 * groups),
            cw_spec(0), cw_spec(groups), cw_spec(2 * groups),
            pl.BlockSpec((None, nsp, hb, 8, span), lambda i, h: (i, 0, slot0 + h, 0, 0)),
            pl.BlockSpec((hb, 8, span), lambda i, h: (h, 0, 0)),
            pl.BlockSpec((1, DN_DK), lambda i, h: (0, 0)),
        ],
        out_specs=pl.BlockSpec((None, s, wb), lambda i, h: (i, 0, h)),
        out_shape=jax.ShapeDtypeStruct((b, s, D_MODEL), BF16),
        scratch_shapes=[
            pltpu.VMEM((hb, s, DN_DK), F32),
            pltpu.VMEM((hb, s, DN_DK), F32),
            pltpu.VMEM((hb, s, DN_DK), F32),
            pltpu.VMEM((2 * hb, nck, DN_DK + DN_CHUNK, DN_DK), BF16),
            pltpu.VMEM((2 * hb, nck, DN_DK, DN_DK), F32),
            pltpu.VMEM((2 * hb, nck, 1, DN_DK), F32),
            pltpu.VMEM((hb, s, DN_DK), F32),
        ],
        compiler_params=pltpu.CompilerParams(
            dimension_semantics=("arbitrary", "arbitrary"), vmem_limit_bytes=VMEM_LIMIT),
        name="deltanet",
    )(big3, big3, big3, big3, conv_w, conv_w, conv_w, gt5, params, norm_g)


def _shortconv_kernel(b_ref, c_ref, x_ref, w_ref, y_ref, *, seq):
    cx = c_ref[...].astype(F32) * x_ref[...].astype(F32)
    rid = lax.broadcasted_iota(jnp.int32, cx.shape, 0)
    cp = jnp.where(rid == 0, 0.0, pltpu.roll(cx, 1, 0))
    cn = jnp.where(rid == seq - 1, 0.0, pltpu.roll(cx, seq - 1, 0))
    conv = w_ref[0:1, :] * cp + w_ref[1:2, :] * cx + w_ref[2:3, :] * cn
    y_ref[...] = (b_ref[...].astype(F32) * conv).astype(y_ref.dtype)


def _shortconv(big3, conv_w, *, tc):
    b, s, _ = big3.shape
    base = 8 * D_MODEL // tc
    per = D_MODEL // tc
    spec = lambda off: pl.BlockSpec((None, s, tc), lambda i, j, off=off: (i, 0, base + off + j))
    return pl.pallas_call(
        functools.partial(_shortconv_kernel, seq=s),
        grid=(b, per),
        in_specs=[spec(0), spec(per), spec(2 * per),
                  pl.BlockSpec((8, tc), lambda i, j: (0, j))],
        out_specs=pl.BlockSpec((None, s, tc), lambda i, j: (i, 0, j)),
        out_shape=jax.ShapeDtypeStruct((b, s, D_MODEL), BF16),
        compiler_params=pltpu.CompilerParams(
            dimension_semantics=("arbitrary", "arbitrary"), vmem_limit_bytes=VMEM_LIMIT),
        name="shortconv",
    )(big3, big3, big3, conv_w)


def _merge_kernel(x_ref, ym_ref, yd_ref, yc_ref, p0_ref, p1_ref, p2_ref, wb_ref, wo_ref, o_ref):
    mixed = _sigmoid(p0_ref[...].astype(F32)) * _dot(ym_ref[...], wb_ref[0])
    mixed += _sigmoid(p1_ref[...].astype(F32)) * _dot(yd_ref[...], wb_ref[1])
    mixed += _sigmoid(p2_ref[...].astype(F32)) * _dot(yc_ref[...], wb_ref[2])
    o_ref[...] = x_ref[...] + _dot(mixed.astype(BF16), wo_ref[...])


def _merge(x2, ym, yd, yc, big2, w_branch, w_out, *, tm):
    m = x2.shape[0]
    row = lambda: pl.BlockSpec((tm, D_MODEL), lambda i: (i, 0))
    pre = lambda n: pl.BlockSpec((tm, D_MODEL), lambda i, n=n: (i, 11 + n))
    return pl.pallas_call(
        _merge_kernel,
        grid=(m // tm,),
        in_specs=[row(), row(), row(), row(), pre(0), pre(1), pre(2),
                  pl.BlockSpec((3, D_MODEL, D_MODEL), lambda i: (0, 0, 0)),
                  pl.BlockSpec((D_MODEL, D_MODEL), lambda i: (0, 0))],
        out_specs=row(),
        out_shape=jax.ShapeDtypeStruct((m, D_MODEL), F32),
        compiler_params=pltpu.CompilerParams(
            dimension_semantics=("arbitrary",), vmem_limit_bytes=VMEM_LIMIT),
        name="merge",
    )(x2, ym, yd, yc, big2, big2, big2, w_branch, w_out)


def _mlp_kernel(x_ref, g_ref, wu_ref, wd_ref, gf_ref, o_ref, *, ff_chunk, final_norm):
    x = x_ref[...]
    h = (x * lax.rsqrt(jnp.mean(x * x, axis=-1, keepdims=True) + EPS) * g_ref[...]).astype(BF16)
    out = x
    for c in range(D_FF // ff_chunk):
        cols = slice(c * ff_chunk, (c + 1) * ff_chunk)
        up = jnp.maximum(_dot(h, wu_ref[:, cols]), 0.0)
        out = out + _dot((up * up).astype(BF16), wd_ref[cols, :])
    if final_norm:
        out = out * lax.rsqrt(jnp.mean(out * out, axis=-1, keepdims=True) + EPS) * gf_ref[...]
    o_ref[...] = out


def _mlp(x2, g, w_up, w_down, g_final, *, tm, final_norm):
    m = x2.shape[0]
    return pl.pallas_call(
        functools.partial(_mlp_kernel, ff_chunk=1024, final_norm=final_norm),
        grid=(m // tm,),
        in_specs=[
            pl.BlockSpec((tm, D_MODEL), lambda i: (i, 0)),
            pl.BlockSpec((1, D_MODEL), lambda i: (0, 0)),
            pl.BlockSpec((D_MODEL, D_FF), lambda i: (0, 0)),
            pl.BlockSpec((D_FF, D_MODEL), lambda i: (0, 0)),
            pl.BlockSpec((1, D_MODEL), lambda i: (0, 0)),
        ],
        out_specs=pl.BlockSpec((tm, D_MODEL), lambda i: (i, 0)),
        out_shape=jax.ShapeDtypeStruct((m, D_MODEL), F32),
        compiler_params=pltpu.CompilerParams(
            dimension_semantics=("arbitrary",), vmem_limit_bytes=VMEM_LIMIT),
        name="mlp",
    )(x2, g, w_up, w_down, g_final)


def _prep_w_in(w_in):
    d = w_in.shape[0]
    c_mg = 4 * D_MODEL
    c_dn = c_mg + 4 * M_HEADS
    c_dg = c_dn + 4 * D_MODEL
    c_sc = c_dg + 4 * DN_HEADS
    w_big = jnp.concatenate([w_in[:, :c_mg], w_in[:, c_dn:c_dg], w_in[:, c_sc:]], axis=1)
    mg = w_in[:, c_mg:c_dn].reshape(d, 4, M_HEADS).transpose(2, 1, 0)
    dg = w_in[:, c_dg:c_sc].reshape(d, 4, DN_HEADS).transpose(2, 1, 0)
    pad = lambda t: jnp.pad(t, ((0, 0), (0, 4), (0, 0))).reshape(-1, d)
    wg_t = jnp.concatenate([pad(mg), pad(dg)], axis=0)
    wg_t = jnp.pad(wg_t, ((0, GATE_ROWS - wg_t.shape[0]), (0, 0)))
    return w_big.astype(BF16), wg_t.astype(BF16)


def _rows8(vals, width):
    h, r = vals.shape
    t = jnp.pad(vals.astype(F32), ((0, 0), (0, 8 - r)))
    return jnp.broadcast_to(t[:, :, None], (h, 8, width))


def _conv_rows(w, width):
    k, c = w.shape
    t = w.astype(F32).reshape(k, c // width, width).transpose(1, 0, 2)
    return jnp.pad(t, ((0, 0), (0, 8 - k), (0, 0)))


def _layer(x2, batch, seq, p, g_final, final_norm):
    m = x2.shape[0]
    span = SPAN
    hb = HEADS_PER_STEP
    w_big, wg_t = _prep_w_in(p["w_in"])
    big, gt = _in_proj(x2, p["norm_mix_g"].reshape(1, -1), w_big, wg_t, tm=1024, tn=1024, span=span)
    big3 = big.reshape(batch, seq, N_BIG)
    gt5 = gt.reshape(batch, seq // span, GATE_ROWS // 8, 8, span)

    gate_b = _rows8(p["m_gate_b"].T, span)
    ym = _mlstm(big3, gt5, gate_b, p["m_norm_g"].reshape(M_HEADS, 1, M_DK), span=span, hb=hb)

    dn_params = _rows8(jnp.concatenate([p["dn_a_log"].T, p["dn_dt_bias"].T], axis=1), span)
    yd = _deltanet(big3, _conv_rows(p["dn_conv_w"], DN_DK), gt5, dn_params,
                   p["dn_norm_g"].reshape(1, -1), span=span, hb=hb)

    sc_w = jnp.pad(p["sc_conv_w"].astype(F32), ((0, 5), (0, 0)))
    yc = _shortconv(big3, sc_w, tc=256)

    x2 = _merge(x2, ym.reshape(m, -1), yd.reshape(m, -1), yc.reshape(m, -1), big,
                p["w_branch"].astype(BF16), p["w_out"].astype(BF16), tm=512)
    x2 = _mlp(x2, p["norm_mlp_g"].reshape(1, -1), p["w_up"].astype(BF16),
              p["w_down"].astype(BF16), g_final.reshape(1, -1), tm=512, final_norm=final_norm)
    return x2


def kernel(x, norm_mix_g, w_in, m_gate_b, m_norm_g, dn_conv_w, dn_a_log, dn_dt_bias,
           dn_norm_g, sc_conv_w, w_branch, w_out, norm_mlp_g, w_up, w_down, norm_final_g):
    batch, seq, d = x.shape
    depth = w_in.shape[0]
    x2 = x.reshape(batch * seq, d)
    stacked = dict(norm_mix_g=norm_mix_g, w_in=w_in, m_gate_b=m_gate_b, m_norm_g=m_norm_g,
                   dn_conv_w=dn_conv_w, dn_a_log=dn_a_log, dn_dt_bias=dn_dt_bias,
                   dn_norm_g=dn_norm_g, sc_conv_w=sc_conv_w, w_branch=w_branch, w_out=w_out,
                   norm_mlp_g=norm_mlp_g, w_up=w_up, w_down=w_down)
    for l in range(depth):
        p = {k: v[l] for k, v in stacked.items()}
        x2 = _layer(x2, batch, seq, p, norm_final_g, final_norm=(l == depth - 1))
    return x2.reshape(batch, seq, d)
```

```python
import functools

import jax
import jax.numpy as jnp
from jax import lax
from jax.experimental import pallas as pl
from jax.experimental.pallas import tpu as pltpu

F32 = jnp.float32
BF16 = jnp.bfloat16

D_MODEL = 1024
M_HEADS = 4
M_DK = 256
DN_HEADS = 8
DN_DK = 128
DN_CHUNK = 64
D_FF = 4096
EPS = 1e-6

SPAN = 256
GATE_ROWS = 128
N_BIG = 14 * D_MODEL
NEG = -1e30
HEADS_PER_STEP = 2
REP_LANES = 128

VMEM_LIMIT = 56 * 1024 * 1024


def _dot(a, b):
    return jnp.dot(a, b, preferred_element_type=F32)


def _dot_nt(a, b):
    return lax.dot_general(a, b, (((1,), (1,)), ((), ())), preferred_element_type=F32)


def _dot_tn(a, b):
    return lax.dot_general(a, b, (((0,), (0,)), ((), ())), preferred_element_type=F32)


def _split3_rows(a, pad_to=None):
    a1 = a.astype(BF16).astype(F32)
    r1 = a - a1
    a2 = r1.astype(BF16).astype(F32)
    a3 = r1 - a2
    parts = [a1, a2, a3]
    if pad_to is not None:
        parts.append(jnp.zeros((pad_to - 3 * a.shape[0], a.shape[1]), F32))
    return jnp.concatenate(parts, axis=0).astype(BF16)


def _scan_max(x, lane, forward):
    n = x.shape[1]
    sh = 1
    while sh < n:
        if forward:
            x = jnp.where(lane >= sh, jnp.maximum(x, pltpu.roll(x, sh, 1)), x)
        else:
            x = jnp.where(lane < n - sh, jnp.maximum(x, pltpu.roll(x, n - sh, 1)), x)
        sh *= 2
    return x


def _rows_times_mask(rows, mask_bf):
    p = _dot(_split3_rows(rows), mask_bf)
    return (p[0:8] + p[8:16]) + p[16:24]


def _rows_to_cols(rows, eye_bf):
    p = _dot_nt(eye_bf, _split3_rows(rows))
    return (p[:, 0:8] + p[:, 8:16]) + p[:, 16:24]


def _softplus(x):
    return jnp.maximum(x, 0.0) + jnp.log(1.0 + jnp.exp(-jnp.abs(x)))


NEG_LOG2E = -1.4426950408889634


def _sigmoid(x):
    return 1.0 / (1.0 + jnp.exp2(x * NEG_LOG2E))


def _l2norm_heads(y, scale):
    outs = []
    for h in range(y.shape[1] // DN_DK):
        t = y[:, h * DN_DK:(h + 1) * DN_DK]
        inv = lax.rsqrt(jnp.sum(t * t, axis=-1, keepdims=True) + EPS)
        outs.append(t * (inv if scale is None else inv * scale))
    return jnp.concatenate(outs, axis=1)


def _in_proj_kernel(x_ref, g_ref, w_ref, wg_ref, cw_ref, big_ref, gt_ref, hn_ref, *, span, seq, tn):
    j = pl.program_id(1)
    dn0 = 4 * D_MODEL // tn
    per_group = D_MODEL // tn
    slab = 2 * DN_DK

    @pl.when(j == 0)
    def _():
        x = x_ref[...]
        y = x * lax.rsqrt(jnp.mean(x * x, axis=-1, keepdims=True) + EPS) * g_ref[...]
        hn = y.astype(BF16)
        hn_ref[...] = hn
        gt = _dot_nt(wg_ref[...], hn)
        for c in range(gt_ref.shape[0]):
            gt_ref[c] = gt[:, c * span:(c + 1) * span]

    in_dn = jnp.logical_and(j >= dn0, j < dn0 + 3 * per_group)

    @pl.when(jnp.logical_not(in_dn))
    def _():
        big_ref[...] = _dot(hn_ref[...], w_ref[...]).astype(big_ref.dtype)

    def dn_tile(kind):
        rb = span
        nrb = seq // rb
        rid = lax.broadcasted_iota(jnp.int32, (8, slab), 0)
        first = rid == 0
        last = rid == 7
        zero_row = jnp.zeros((1, slab), F32)
        for s in range(tn // slab):
            cols = slice(s * slab, (s + 1) * slab)
            cw = cw_ref[:, cols]
            prev_last = zero_row
            cur = _dot(hn_ref[0:rb, :], w_ref[:, cols])
            for r in range(nrb):
                if r + 1 < nrb:
                    nxt = _dot(hn_ref[(r + 1) * rb:(r + 2) * rb, :], w_ref[:, cols])
                    next_first = nxt[0:1]
                else:
                    nxt = None
                    next_first = zero_row
                xp = pltpu.roll(cur, 1, 0)
                xp = jnp.concatenate([jnp.where(first, prev_last, xp[0:8]), xp[8:]], axis=0)
                xn = pltpu.roll(cur, rb - 1, 0)
                xn = jnp.concatenate([xn[:rb - 8], jnp.where(last, next_first, xn[rb - 8:])], axis=0)
                y = cw[0:1] * xp + cw[1:2] * cur + cw[2:3] * xn
                y = y * _sigmoid(y)
                if kind == 0:
                    y = _l2norm_heads(y, DN_DK ** -0.5)
                elif kind == 1:
                    y = _l2norm_heads(y, None)
                big_ref[r * rb:(r + 1) * rb, cols] = y.astype(big_ref.dtype)
                prev_last = cur[rb - 1:rb]
                cur = nxt

    for kind in range(3):
        pl.when(jnp.logical_and(j >= dn0 + kind * per_group, j < dn0 + (kind + 1) * per_group))(
            functools.partial(dn_tile, kind))


def _in_proj(x2, g, w_big, wg_t, conv_w, *, seq, tn, span):
    m = x2.shape[0]
    n = w_big.shape[1]
    tm = seq
    dn0 = 4 * D_MODEL // tn
    n_dn = 3 * D_MODEL // tn
    return pl.pallas_call(
        functools.partial(_in_proj_kernel, span=span, seq=seq, tn=tn),
        grid=(m // tm, n // tn),
        in_specs=[
            pl.BlockSpec((tm, D_MODEL), lambda i, j: (i, 0)),
            pl.BlockSpec((1, D_MODEL), lambda i, j: (0, 0)),
            pl.BlockSpec((D_MODEL, tn), lambda i, j: (0, j)),
            pl.BlockSpec((GATE_ROWS, D_MODEL), lambda i, j: (0, 0)),
            pl.BlockSpec((8, tn), lambda i, j: (0, jnp.clip(j - dn0, 0, n_dn - 1))),
        ],
        out_specs=[
            pl.BlockSpec((tm, tn), lambda i, j: (i, j)),
            pl.BlockSpec((tm // span, GATE_ROWS, span), lambda i, j: (i, 0, 0)),
        ],
        out_shape=[
            jax.ShapeDtypeStruct((m, n), BF16),
            jax.ShapeDtypeStruct((m // span, GATE_ROWS, span), F32),
        ],
        scratch_shapes=[pltpu.VMEM((tm, D_MODEL), BF16)],
        compiler_params=pltpu.CompilerParams(
            dimension_semantics=("arbitrary", "arbitrary"), vmem_limit_bytes=VMEM_LIMIT),
        name="in_proj",
    )(x2, g, w_big, wg_t, conv_w)


def _mlstm_kernel(q_ref, k_ref, v_ref, o_ref, gt_ref, gb_ref, ng_ref, y_ref,
                  c_ref, hm_ref, rep_ref, rrow_ref, *, seq, span, hb):
    nc = seq // span
    scale = M_DK ** -0.5
    rw = REP_LANES
    chains = [(hh, fwd) for hh in range(hb) for fwd in (True, False)]
    nch = len(chains)
    each = lambda f, *ls: [f(*a) for a in zip(*ls)]
    wide = lambda t: jnp.concatenate([t] * (span // rw), axis=1)

    c_ref[...] = jnp.zeros_like(c_ref)
    hm_ref[...] = jnp.zeros_like(hm_ref)

    def gates(c, carry):
        ri = lax.broadcasted_iota(jnp.int32, (span, span), 0)
        ci = lax.broadcasted_iota(jnp.int32, (span, span), 1)
        eye_bf = (ri == ci).astype(BF16)
        cum = {True: (ri <= ci).astype(BF16), False: (ri >= ci).astype(BF16)}
        lane = lax.broadcasted_iota(jnp.int32, (8, span), 1)
        ek = lax.broadcasted_iota(jnp.int32, (rw, 3 * rw), 0)
        el = lax.broadcasted_iota(jnp.int32, (rw, 3 * rw), 1)
        pick_bf = jnp.logical_and(ek < 24, (ek & 7) == (el >> 7)).astype(BF16)
        zero7 = jnp.zeros((7, span), F32)
        zero6 = jnp.zeros((6, span), F32)
        zero5 = jnp.zeros((5, span), F32)
        work = [(n, hh, f, c * gpi + j) for j in range(gpi) for n, (hh, f) in enumerate(chains)]
        g8 = [gt_ref[x, hh] + gb_ref[hh] for _, hh, _, x in work]
        ig = [g[0:1] if f else g[2:3] for g, (_, _, f, _) in zip(g8, work)]
        lf = [-_softplus(-(g[1:2] if f else g[3:4])) for g, (_, _, f, _) in zip(g8, work)]
        b = [_rows_times_mask(jnp.concatenate([l, zero7], axis=0), cum[f])[0:1]
             for l, (_, _, f, _) in zip(lf, work)]
        r = each(lambda i, bb: i - bb, ig, b)
        mx = [_scan_max(jnp.broadcast_to(x, (8, span)), lane, f)[0:1] for x, (_, _, f, _) in zip(r, work)]
        pieces = [_split3_rows(jnp.concatenate([bb, i, m, zero5], axis=0), pad_to=rw)
                  for bb, i, m in zip(b, ig, mx)]
        cols = [_dot_nt(eye_bf, p) for p in pieces]
        rep = [_dot(t.astype(BF16), pick_bf) for t in cols]
        for w, (n, _, _, x) in enumerate(work):
            rep_ref[n, x] = rep[w]
            rrow_ref[n, x] = jnp.concatenate([r[w], lf[w], zero6], axis=0)
        return carry

    gpi = 4 if nc % 4 == 0 else 1
    lax.fori_loop(0, nc // gpi, gates, 0)

    def body(c, carry):
        n_st = list(carry[:nch])
        m_st = list(carry[nch:])
        ri = lax.broadcasted_iota(jnp.int32, (span, span), 0)
        ci = lax.broadcasted_iota(jnp.int32, (span, span), 1)
        mask = {True: ci <= ri, False: ci >= ri}
        ones_bf = jnp.ones((span, rw), BF16)
        cc = [c if f else nc - 1 - c for _, f in chains]
        rows = [pl.ds(pl.multiple_of(x * span, span), span) for x in cc]
        hcols = [slice(hh * M_DK, (hh + 1) * M_DK) for hh, _ in chains]
        rep = [rep_ref[n, x] for n, x in enumerate(cc)]
        rr = [rrow_ref[n, x] for n, x in enumerate(cc)]
        b_rep = [t[:, 0:rw] for t in rep]
        ig_rep = [t[:, rw:2 * rw] for t in rep]
        mx_rep = [t[:, 2 * rw:3 * rw] for t in rep]
        r_row = [t[0:1] for t in rr]
        b_last = [jnp.sum(t[1:2], axis=1, keepdims=True) for t in rr]
        q = [(q_ref[rw_, cs].astype(F32) * scale).astype(BF16) for rw_, cs in zip(rows, hcols)]
        k = [k_ref[rw_, cs] for rw_, cs in zip(rows, hcols)]
        v = [v_ref[rw_, cs] for rw_, cs in zip(rows, hcols)]
        s = each(_dot_nt, q, k)
        c_st = [c_ref[n] for n in range(nch)]
        qc = each(lambda a, t: _dot(a, t.astype(BF16)), q, c_st)
        qn = each(lambda a, t: _dot_nt(a, jnp.broadcast_to(t, (rw, M_DK)).astype(BF16)), q, n_st)
        c1 = each(lambda m, t: -jnp.maximum(m, t), m_st, mx_rep)
        inter = each(lambda m, t: jnp.exp(m + t), m_st, c1)
        e_negm = each(lambda t, bb: jnp.exp(t - bb), c1, b_rep)
        d = [jnp.exp(jnp.where(mask[f], rr_ + wide(t), NEG)) for rr_, t, (_, f) in zip(r_row, c1, chains)]
        sc = each(lambda a, t: (a * t).astype(BF16), s, d)
        scv = each(_dot, sc, v)
        rs = [_dot(t, ones_bf) for t in sc]
        den = each(lambda i, a, t: i * a + t, inter, qn, rs)
        inv = each(lambda t, e: 1.0 / jnp.maximum(jnp.abs(t), e), den, e_negm)
        for n, (hh, _) in enumerate(chains):
            hm_ref[hh, rows[n], :] += wide(inter[n] * inv[n]) * qc[n] + wide(inv[n]) * scv[n]

        a_rep = each(lambda bl, bb, i: bl - bb + i, b_last, b_rep, ig_rep)
        m_new = each(lambda bl, m, a: jnp.maximum(
            bl + m, jnp.max(jnp.max(a, axis=0, keepdims=True), axis=1, keepdims=True)),
            b_last, m_st, a_rep)
        wgt = each(lambda a, m: jnp.exp(a - m), a_rep, m_new)
        dec = each(lambda bl, m, mn: jnp.exp(bl + m - mn), b_last, m_st, m_new)
        kw = each(lambda a, w: a.astype(F32) * wide(w), k, wgt)
        upd = each(lambda a, t: _dot_tn(a.astype(BF16), t), kw, v)
        for n in range(nch):
            c_ref[n] = dec[n] * c_st[n] + upd[n]
        n_new = each(lambda dd, t, a: dd * t + jnp.sum(a, axis=0, keepdims=True), dec, n_st, kw)
        return tuple(n_new) + tuple(m_new)

    n0 = jnp.zeros((1, M_DK), F32)
    m0 = jnp.zeros((1, 1), F32)
    lax.fori_loop(0, nc, body, (n0,) * nch + (m0,) * nch)

    def finish(t, carry):
        rows = pl.ds(pl.multiple_of(t * span, span), span)
        for hh in range(hb):
            cols_h = slice(hh * M_DK, (hh + 1) * M_DK)
            h = hm_ref[hh, rows, :]
            y = h * lax.rsqrt(jnp.mean(h * h, axis=-1, keepdims=True) + EPS) * ng_ref[hh]
            y_ref[rows, cols_h] = (_sigmoid(o_ref[rows, cols_h].astype(F32)) * y).astype(y_ref.dtype)
        return carry

    lax.fori_loop(0, nc, finish, 0)


def _mlstm(big3, gt5, gate_b, norm_g, *, span, hb):
    b, s, _ = big3.shape
    nc = s // span
    wb = hb * M_DK
    groups = M_HEADS // hb
    q_spec = lambda off: pl.BlockSpec((None, s, wb), lambda i, h, off=off: (i, 0, off + h))
    return pl.pallas_call(
        functools.partial(_mlstm_kernel, seq=s, span=span, hb=hb),
        grid=(b, groups),
        in_specs=[
            q_spec(0), q_spec(groups), q_spec(2 * groups), q_spec(3 * groups),
            pl.BlockSpec((None, nc, hb, 8, span), lambda i, h: (i, 0, h, 0, 0)),
            pl.BlockSpec((hb, 8, span), lambda i, h: (h, 0, 0)),
            pl.BlockSpec((hb, 1, M_DK), lambda i, h: (h, 0, 0)),
        ],
        out_specs=pl.BlockSpec((None, s, wb), lambda i, h: (i, 0, h)),
        out_shape=jax.ShapeDtypeStruct((b, s, D_MODEL), BF16),
        scratch_shapes=[
            pltpu.VMEM((2 * hb, M_DK, M_DK), F32),
            pltpu.VMEM((hb, s, M_DK), F32),
            pltpu.VMEM((2 * hb, nc, span, 3 * REP_LANES), F32),
            pltpu.VMEM((2 * hb, nc, 8, span), F32),
        ],
        compiler_params=pltpu.CompilerParams(
            dimension_semantics=("arbitrary", "arbitrary"), vmem_limit_bytes=VMEM_LIMIT),
        name="mlstm",
    )(big3, big3, big3, big3, gt5, gate_b, norm_g)


def _log2(n):
    return n.bit_length() - 1


def _deltanet_kernel(q_ref, k_ref, v_ref, z_ref, gt_ref, p_ref,
                     ng_ref, y_ref, pq_ref, nn_ref, gl_ref, acc_ref,
                     *, seq, span, hb):
    nsp = seq // span
    lc = DN_CHUNK
    nck = seq // lc
    per = span // lc
    sh = _log2(lc)
    spi = 2

    def fold(full):
        lane_grp = lax.broadcasted_iota(jnp.int32, (lc, span), 1) >> sh
        out = full[(per - 1) * lc:per * lc]
        for c in range(per - 2, -1, -1):
            out = jnp.where(lane_grp == c, full[c * lc:(c + 1) * lc], out)
        return out

    def widen(col):
        lane_grp = lax.broadcasted_iota(jnp.int32, (lc, span), 1) >> sh
        out = jnp.broadcast_to(col[(per - 1) * lc:per * lc], (lc, span))
        for c in range(per - 2, -1, -1):
            out = jnp.where(lane_grp == c, jnp.broadcast_to(col[c * lc:(c + 1) * lc], (lc, span)), out)
        return out

    def blockdiag(wide_bf):
        ri = lax.broadcasted_iota(jnp.int32, (span, span), 0)
        ci = lax.broadcasted_iota(jnp.int32, (span, span), 1)
        tiled = jnp.concatenate([wide_bf] * per, axis=0)
        return jnp.where((ri >> sh) == (ci >> sh), tiled, jnp.zeros_like(tiled))

    def wide_inverse(aws, ri, cj):
        eye = (ri == cj).astype(F32)
        same16 = (ri >> 4) == (cj >> 4)
        ns = [jnp.where(same16, aw, 0.0).astype(BF16) for aw in aws]
        xs = [eye - n.astype(F32) for n in ns]
        ns = [_dot(n, blockdiag(n)).astype(BF16) for n in ns]
        for _ in range(2):
            rs = [_dot(jnp.concatenate([x.astype(BF16), n], axis=0), blockdiag(n)) for x, n in zip(xs, ns)]
            xs = [x + r[:lc] for x, r in zip(xs, rs)]
            ns = [r[lc:].astype(BF16) for r in rs]
        xs = [x + _dot(x.astype(BF16), blockdiag(n)) for x, n in zip(xs, ns)]
        lvl = 4
        while (1 << lvl) < lc:
            inner = (ri >> lvl) == (cj >> lvl)
            outer = (ri >> (lvl + 1)) == (cj >> (lvl + 1))
            pick = jnp.logical_and(outer, jnp.logical_not(inner))
            xb = [x.astype(BF16) for x in xs]
            xo = [_dot(x, blockdiag(jnp.where(pick, aw, 0.0).astype(BF16))).astype(BF16)
                  for x, aw in zip(xb, aws)]
            xs = [x - _dot(t, blockdiag(b)) for x, t, b in zip(xs, xo, xb)]
            lvl += 1
        return xs

    def prep(it, carry):
        ri_f = lax.broadcasted_iota(jnp.int32, (span, span), 0)
        ci_f = lax.broadcasted_iota(jnp.int32, (span, span), 1)
        same_f = (ri_f >> sh) == (ci_f >> sh)
        eye_bf = (ri_f == ci_f).astype(BF16)
        cum_bf = jnp.concatenate([jnp.logical_and(same_f, ri_f <= ci_f).astype(BF16),
                                  jnp.logical_and(same_f, ri_f >= ci_f).astype(BF16)], axis=1)
        ri = lax.broadcasted_iota(jnp.int32, (lc, span), 0)
        cj = lax.broadcasted_iota(jnp.int32, (lc, span), 1) & (lc - 1)
        zero6 = jnp.zeros((6, span), F32)
        zero2 = jnp.zeros((2, span), F32)

        units = [(hh, it * spi + j) for j in range(spi) for hh in range(hb)]
        rows_u = [pl.ds(pl.multiple_of(s * span, span), span) for _, s in units]
        hcols = [slice(hh * DN_DK, (hh + 1) * DN_DK) for hh, _ in units]
        k_bf = [k_ref[r, cs] for r, cs in zip(rows_u, hcols)]
        q_bf = [q_ref[r, cs] for r, cs in zip(rows_u, hcols)]
        kf = [t.astype(F32) for t in k_bf]
        qf = [t.astype(F32) for t in q_bf]
        vf = [v_ref[r, cs].astype(F32) for r, cs in zip(rows_u, hcols)]
        kq = [_dot_nt(jnp.concatenate([k, q], axis=0), k) for k, q in zip(k_bf, q_bf)]
        kkw = [fold(t[:span]) for t in kq]
        qkw = [fold(t[span:]) for t in kq]

        g8 = [gt_ref[s, hh] for hh, s in units]
        pr = [p_ref[hh] for hh, _ in units]
        beta_f = [_sigmoid(g[0:1]) for g in g8]
        beta_b = [_sigmoid(g[2:3]) for g in g8]
        g_f = [-jnp.exp(p[0:1]) * _softplus(g[1:2] + p[2:3]) for g, p in zip(g8, pr)]
        g_b = [-jnp.exp(p[1:2]) * _softplus(g[3:4] + p[3:4]) for g, p in zip(g8, pr)]
        g2 = [jnp.concatenate([a, b, zero6], axis=0) for a, b in zip(g_f, g_b)]
        cum = [_rows_times_mask(g, cum_bf) for g in g2]
        cum_fwd = [c[:, :span] for c in cum]
        cum_rev = [c[:, span:] for c in cum]
        tot = [a + b - g for a, b, g in zip(cum_fwd, cum_rev, g2)]
        cols = [_rows_to_cols(jnp.concatenate([bf, cf[0:1], t[0:1], bb, cr[1:2], t[1:2], zero2], axis=0),
                              eye_bf)
                for bf, cf, t, bb, cr in zip(beta_f, cum_fwd, tot, beta_b, cum_rev)]

        chains = [(u, fwd) for u in range(len(units)) for fwd in (True, False)]
        col = lambda u, fwd, k: cols[u][:, (0 if fwd else 3) + k:(0 if fwd else 3) + k + 1]
        beta_col = [col(u, f, 0) for u, f in chains]
        gc_col = [col(u, f, 1) for u, f in chains]
        tot_col = [col(u, f, 2) for u, f in chains]
        gc_row = [cum_fwd[u][0:1] if f else cum_rev[u][1:2] for u, f in chains]
        tot_row = [tot[u][0:1] if f else tot[u][1:2] for u, f in chains]
        decay = [jnp.where((cj <= ri) if f else (cj >= ri), jnp.exp(widen(gc) - gr), 0.0)
                 for (u, f), gc, gr in zip(chains, gc_col, gc_row)]
        attn = [qkw[u] * d for (u, f), d in zip(chains, decay)]
        aws = [jnp.where((cj < ri) if f else (cj > ri), kkw[u] * d * widen(bc), 0.0)
               for (u, f), d, bc in zip(chains, decay, beta_col)]
        tws = wide_inverse(aws, ri, cj)
        e_gc = [jnp.exp(gc) for gc in gc_col]
        rhs = [jnp.concatenate([(kf[u] * (bc * e)).astype(BF16), (vf[u] * bc).astype(BF16)], axis=1)
               for (u, f), bc, e in zip(chains, beta_col, e_gc)]
        wu_bf = [_dot(blockdiag(t.astype(BF16)), r).astype(BF16) for t, r in zip(tws, rhs)]
        awu = [_dot(blockdiag(a.astype(BF16)), w) for a, w in zip(attn, wu_bf)]
        qp = [(qf[u] * e - t[:, :DN_DK]).astype(BF16) for (u, f), e, t in zip(chains, e_gc, awu)]
        kd = [(kf[u] * jnp.exp(tc - gc)).astype(BF16) for (u, f), tc, gc in zip(chains, tot_col, gc_col)]
        gl_row = [jnp.exp(t) for t in tot_row]
        for n, (u, f) in enumerate(chains):
            hh, s = units[u]
            idx = 2 * hh + (0 if f else 1)
            for c in range(per):
                cr = slice(c * lc, (c + 1) * lc)
                pn = _dot_tn(kd[n][cr], wu_bf[n][cr])
                chunk = s * per + c
                pq_ref[idx, chunk, 0:DN_DK, :] = pn[:, :DN_DK].astype(BF16)
                pq_ref[idx, chunk, DN_DK:DN_DK + lc, :] = qp[n][cr]
                nn_ref[idx, chunk] = pn[:, DN_DK:]
                gl_ref[idx, chunk] = jnp.broadcast_to(gl_row[n][:, c * lc:c * lc + 1], (1, DN_DK))
        for u, (hh, s) in enumerate(units):
            acc_ref[hh, rows_u[u], :] = awu[2 * u][:, DN_DK:] + awu[2 * u + 1][:, DN_DK:]
        return carry

    lax.fori_loop(0, nsp // spi, prep, 0)

    def step(c, st, hh, idx):
        r = _dot(pq_ref[idx, c], st.astype(BF16))
        rows = pl.ds(pl.multiple_of(c * lc, lc), lc)
        acc_ref[hh, rows, :] += r[DN_DK:]
        return gl_ref[idx, c] * st - r[:DN_DK] + nn_ref[idx, c]

    def body(c, carry):
        out = []
        for hh in range(hb):
            out.append(step(c, carry[2 * hh], hh, 2 * hh))
            out.append(step(nck - 1 - c, carry[2 * hh + 1], hh, 2 * hh + 1))
        return tuple(out)

    s0 = jnp.zeros((DN_DK, DN_DK), F32)
    lax.fori_loop(0, nck, body, (s0,) * (2 * hb))

    def finish(t, carry):
        rows = pl.ds(pl.multiple_of(t * span, span), span)
        for hh in range(hb):
            cols_h = slice(hh * DN_DK, (hh + 1) * DN_DK)
            o = acc_ref[hh, rows, :]
            y = o * lax.rsqrt(jnp.mean(o * o, axis=-1, keepdims=True) + EPS) * ng_ref[...]
            z = z_ref[rows, cols_h].astype(F32)
            y_ref[rows, cols_h] = (y * (z * _sigmoid(z))).astype(y_ref.dtype)
        return carry

    lax.fori_loop(0, nsp, finish, 0)


def _deltanet(big3, gt5, params, norm_g, *, span, hb):
    b, s, _ = big3.shape
    nsp = s // span
    nck = s // DN_CHUNK
    wb = hb * DN_DK
    groups = DN_HEADS // hb
    base = 4 * D_MODEL // wb
    slot0 = M_HEADS // hb
    q_spec = lambda off: pl.BlockSpec((None, s, wb), lambda i, h, off=off: (i, 0, base + off + h))
    return pl.pallas_call(
        functools.partial(_deltanet_kernel, seq=s, span=span, hb=hb),
        grid=(b, groups),
        in_specs=[
            q_spec(0), q_spec(groups), q_spec(2 * groups), q_spec(3 * groups),
            pl.BlockSpec((None, nsp, hb, 8, span), lambda i, h: (i, 0, slot0 + h, 0, 0)),
            pl.BlockSpec((hb, 8, span), lambda i, h: (h, 0, 0)),
            pl.BlockSpec((1, DN_DK), lambda i, h: (0, 0)),
        ],
        out_specs=pl.BlockSpec((None, s, wb), lambda i, h: (i, 0, h)),
        out_shape=jax.ShapeDtypeStruct((b, s, D_MODEL), BF16),
        scratch_shapes=[
            pltpu.VMEM((2 * hb, nck, DN_DK + DN_CHUNK, DN_DK), BF16),
            pltpu.VMEM((2 * hb, nck, DN_DK, DN_DK), F32),
            pltpu.VMEM((2 * hb, nck, 1, DN_DK), F32),
            pltpu.VMEM((hb, s, DN_DK), F32),
        ],
        compiler_params=pltpu.CompilerParams(
            dimension_semantics=("arbitrary", "arbitrary"), vmem_limit_bytes=VMEM_LIMIT),
        name="deltanet",
    )(big3, big3, big3, big3, gt5, params, norm_g)


def _shortconv_kernel(b_ref, c_ref, x_ref, w_ref, y_ref, *, seq):
    cx = c_ref[...].astype(F32) * x_ref[...].astype(F32)
    rid = lax.broadcasted_iota(jnp.int32, cx.shape, 0)
    cp = jnp.where(rid == 0, 0.0, pltpu.roll(cx, 1, 0))
    cn = jnp.where(rid == seq - 1, 0.0, pltpu.roll(cx, seq - 1, 0))
    conv = w_ref[0:1, :] * cp + w_ref[1:2, :] * cx + w_ref[2:3, :] * cn
    y_ref[...] = (b_ref[...].astype(F32) * conv).astype(y_ref.dtype)


def _shortconv(big3, conv_w, *, tc):
    b, s, _ = big3.shape
    base = 8 * D_MODEL // tc
    per = D_MODEL // tc
    spec = lambda off: pl.BlockSpec((None, s, tc), lambda i, j, off=off: (i, 0, base + off + j))
    return pl.pallas_call(
        functools.partial(_shortconv_kernel, seq=s),
        grid=(b, per),
        in_specs=[spec(0), spec(per), spec(2 * per),
                  pl.BlockSpec((8, tc), lambda i, j: (0, j))],
        out_specs=pl.BlockSpec((None, s, tc), lambda i, j: (i, 0, j)),
        out_shape=jax.ShapeDtypeStruct((b, s, D_MODEL), BF16),
        compiler_params=pltpu.CompilerParams(
            dimension_semantics=("arbitrary", "arbitrary"), vmem_limit_bytes=VMEM_LIMIT),
        name="shortconv",
    )(big3, big3, big3, conv_w)


def _merge_kernel(x_ref, ym_ref, yd_ref, yc_ref, p0_ref, p1_ref, p2_ref, wb_ref, wo_ref, o_ref):
    mixed = _sigmoid(p0_ref[...].astype(F32)) * _dot(ym_ref[...], wb_ref[0])
    mixed += _sigmoid(p1_ref[...].astype(F32)) * _dot(yd_ref[...], wb_ref[1])
    mixed += _sigmoid(p2_ref[...].astype(F32)) * _dot(yc_ref[...], wb_ref[2])
    o_ref[...] = x_ref[...] + _dot(mixed.astype(BF16), wo_ref[...])


def _merge(x2, ym, yd, yc, big2, w_branch, w_out, *, tm):
    m = x2.shape[0]
    row = lambda: pl.BlockSpec((tm, D_MODEL), lambda i: (i, 0))
    pre = lambda n: pl.BlockSpec((tm, D_MODEL), lambda i, n=n: (i, 11 + n))
    return pl.pallas_call(
        _merge_kernel,
        grid=(m // tm,),
        in_specs=[row(), row(), row(), row(), pre(0), pre(1), pre(2),
                  pl.BlockSpec((3, D_MODEL, D_MODEL), lambda i: (0, 0, 0)),
                  pl.BlockSpec((D_MODEL, D_MODEL), lambda i: (0, 0))],
        out_specs=row(),
        out_shape=jax.ShapeDtypeStruct((m, D_MODEL), F32),
        compiler_params=pltpu.CompilerParams(
            dimension_semantics=("arbitrary",), vmem_limit_bytes=VMEM_LIMIT),
        name="merge",
    )(x2, ym, yd, yc, big2, big2, big2, w_branch, w_out)


def _mlp_kernel(x_ref, g_ref, wu_ref, wd_ref, gf_ref, o_ref, *, ff_chunk, final_norm):
    x = x_ref[...]
    h = (x * lax.rsqrt(jnp.mean(x * x, axis=-1, keepdims=True) + EPS) * g_ref[...]).astype(BF16)
    out = x
    for c in range(D_FF // ff_chunk):
        cols = slice(c * ff_chunk, (c + 1) * ff_chunk)
        up = jnp.maximum(_dot(h, wu_ref[:, cols]), 0.0)
        out = out + _dot((up * up).astype(BF16), wd_ref[cols, :])
    if final_norm:
        out = out * lax.rsqrt(jnp.mean(out * out, axis=-1, keepdims=True) + EPS) * gf_ref[...]
    o_ref[...] = out


def _mlp(x2, g, w_up, w_down, g_final, *, tm, final_norm):
    m = x2.shape[0]
    return pl.pallas_call(
        functools.partial(_mlp_kernel, ff_chunk=1024, final_norm=final_norm),
        grid=(m // tm,),
        in_specs=[
            pl.BlockSpec((tm, D_MODEL), lambda i: (i, 0)),
            pl.BlockSpec((1, D_MODEL), lambda i: (0, 0)),
            pl.BlockSpec((D_MODEL, D_FF), lambda i: (0, 0)),
            pl.BlockSpec((D_FF, D_MODEL), lambda i: (0, 0)),
            pl.BlockSpec((1, D_MODEL), lambda i: (0, 0)),
        ],
        out_specs=pl.BlockSpec((tm, D_MODEL), lambda i: (i, 0)),
        out_shape=jax.ShapeDtypeStruct((m, D_MODEL), F32),
        compiler_params=pltpu.CompilerParams(
            dimension_semantics=("arbitrary",), vmem_limit_bytes=VMEM_LIMIT),
        name="mlp",
    )(x2, g, w_up, w_down, g_final)


def _prep_w_in(w_in):
    d = w_in.shape[0]
    c_mg = 4 * D_MODEL
    c_dn = c_mg + 4 * M_HEADS
    c_dg = c_dn + 4 * D_MODEL
    c_sc = c_dg + 4 * DN_HEADS
    w_big = jnp.concatenate([w_in[:, :c_mg], w_in[:, c_dn:c_dg], w_in[:, c_sc:]], axis=1)
    mg = w_in[:, c_mg:c_dn].reshape(d, 4, M_HEADS).transpose(2, 1, 0)
    dg = w_in[:, c_dg:c_sc].reshape(d, 4, DN_HEADS).transpose(2, 1, 0)
    pad = lambda t: jnp.pad(t, ((0, 0), (0, 4), (0, 0))).reshape(-1, d)
    wg_t = jnp.concatenate([pad(mg), pad(dg)], axis=0)
    wg_t = jnp.pad(wg_t, ((0, GATE_ROWS - wg_t.shape[0]), (0, 0)))
    return w_big.astype(BF16), wg_t.astype(BF16)


def _rows8(vals, width):
    h, r = vals.shape
    t = jnp.pad(vals.astype(F32), ((0, 0), (0, 8 - r)))
    return jnp.broadcast_to(t[:, :, None], (h, 8, width))


def _layer(x2, batch, seq, p, g_final, final_norm):
    m = x2.shape[0]
    span = SPAN
    hb = HEADS_PER_STEP
    w_big, wg_t = _prep_w_in(p["w_in"])
    dn_conv = jnp.pad(p["dn_conv_w"].astype(F32), ((0, 5), (0, 0)))
    big, gt = _in_proj(x2, p["norm_mix_g"].reshape(1, -1), w_big, wg_t, dn_conv,
                       seq=seq, tn=1024, span=span)
    big3 = big.reshape(batch, seq, N_BIG)
    gt5 = gt.reshape(batch, seq // span, GATE_ROWS // 8, 8, span)

    gate_b = _rows8(p["m_gate_b"].T, span)
    ym = _mlstm(big3, gt5, gate_b, p["m_norm_g"].reshape(M_HEADS, 1, M_DK), span=span, hb=hb)

    dn_params = _rows8(jnp.concatenate([p["dn_a_log"].T, p["dn_dt_bias"].T], axis=1), span)
    yd = _deltanet(big3, gt5, dn_params, p["dn_norm_g"].reshape(1, -1), span=span, hb=hb)

    sc_w = jnp.pad(p["sc_conv_w"].astype(F32), ((0, 5), (0, 0)))
    yc = _shortconv(big3, sc_w, tc=256)

    x2 = _merge(x2, ym.reshape(m, -1), yd.reshape(m, -1), yc.reshape(m, -1), big,
                p["w_branch"].astype(BF16), p["w_out"].astype(BF16), tm=512)
    x2 = _mlp(x2, p["norm_mlp_g"].reshape(1, -1), p["w_up"].astype(BF16),
              p["w_down"].astype(BF16), g_final.reshape(1, -1), tm=512, final_norm=final_norm)
    return x2


def kernel(x, norm_mix_g, w_in, m_gate_b, m_norm_g, dn_conv_w, dn_a_log, dn_dt_bias,
           dn_norm_g, sc_conv_w, w_branch, w_out, norm_mlp_g, w_up, w_down, norm_final_g):
    batch, seq, d = x.shape
    depth = w_in.shape[0]
    x2 = x.reshape(batch * seq, d)
    stacked = dict(norm_mix_g=norm_mix_g, w_in=w_in, m_gate_b=m_gate_b, m_norm_g=m_norm_g,
                   dn_conv_w=dn_conv_w, dn_a_log=dn_a_log, dn_dt_bias=dn_dt_bias,
                   dn_norm_g=dn_norm_g, sc_conv_w=sc_conv_w, w_branch=w_branch, w_out=w_out,
                   norm_mlp_g=norm_mlp_g, w_up=w_up, w_down=w_down)
    for l in range(depth):
        p = {k: v[l] for k, v in stacked.items()}
        x2 = _layer(x2, batch, seq, p, norm_final_g, final_norm=(l == depth - 1))
    return x2.reshape(batch, seq, d)
```

```python
import functools

import jax
import jax.numpy as jnp
from jax import lax
from jax.experimental import pallas as pl
from jax.experimental.pallas import tpu as pltpu

F32 = jnp.float32
BF16 = jnp.bfloat16

D_MODEL = 1024
M_HEADS = 4
M_DK = 256
DN_HEADS = 8
DN_DK = 128
DN_CHUNK = 64
D_FF = 4096
EPS = 1e-6

SPAN = 256
GATE_ROWS = 128
N_BIG = 14 * D_MODEL
NEG = -1e30
HEADS_PER_STEP = 2
REP_LANES = 128

VMEM_LIMIT = 56 * 1024 * 1024


def _dot(a, b):
    return jnp.dot(a, b, preferred_element_type=F32)


def _dot_nt(a, b):
    return lax.dot_general(a, b, (((1,), (1,)), ((), ())), preferred_element_type=F32)


def _dot_tn(a, b):
    return lax.dot_general(a, b, (((0,), (0,)), ((), ())), preferred_element_type=F32)


def _split3_rows(a, pad_to=None):
    a1 = a.astype(BF16).astype(F32)
    r1 = a - a1
    a2 = r1.astype(BF16).astype(F32)
    a3 = r1 - a2
    parts = [a1, a2, a3]
    if pad_to is not None:
        parts.append(jnp.zeros((pad_to - 3 * a.shape[0], a.shape[1]), F32))
    return jnp.concatenate(parts, axis=0).astype(BF16)


def _scan_max(x, lane, forward):
    n = x.shape[1]
    sh = 1
    while sh < n:
        if forward:
            x = jnp.where(lane >= sh, jnp.maximum(x, pltpu.roll(x, sh, 1)), x)
        else:
            x = jnp.where(lane < n - sh, jnp.maximum(x, pltpu.roll(x, n - sh, 1)), x)
        sh *= 2
    return x


def _rows_times_mask(rows, mask_bf):
    p = _dot(_split3_rows(rows), mask_bf)
    return (p[0:8] + p[8:16]) + p[16:24]


def _rows_to_cols(rows, eye_bf):
    p = _dot_nt(eye_bf, _split3_rows(rows))
    return (p[:, 0:8] + p[:, 8:16]) + p[:, 16:24]


def _softplus(x):
    return jnp.maximum(x, 0.0) + jnp.log(1.0 + jnp.exp(-jnp.abs(x)))


NEG_LOG2E = -1.4426950408889634


def _sigmoid(x):
    return 1.0 / (1.0 + jnp.exp2(x * NEG_LOG2E))


def _l2norm_heads(y, scale):
    outs = []
    for h in range(y.shape[1] // DN_DK):
        t = y[:, h * DN_DK:(h + 1) * DN_DK]
        inv = lax.rsqrt(jnp.sum(t * t, axis=-1, keepdims=True) + EPS)
        outs.append(t * (inv if scale is None else inv * scale))
    return jnp.concatenate(outs, axis=1)


def _in_proj_kernel(x_ref, g_ref, w_ref, wg_ref, cw_ref, big_ref, gt_ref, hn_ref, *, span, seq, tn):
    j = pl.program_id(1)
    dn0 = 4 * D_MODEL // tn
    per_group = D_MODEL // tn
    slab = 2 * DN_DK

    @pl.when(j == 0)
    def _():
        x = x_ref[...]
        y = x * lax.rsqrt(jnp.mean(x * x, axis=-1, keepdims=True) + EPS) * g_ref[...]
        hn = y.astype(BF16)
        hn_ref[...] = hn
        gt = _dot_nt(wg_ref[...], hn)
        for c in range(gt_ref.shape[0]):
            gt_ref[c] = gt[:, c * span:(c + 1) * span]

    in_dn = jnp.logical_and(j >= dn0, j < dn0 + 3 * per_group)

    @pl.when(jnp.logical_not(in_dn))
    def _():
        big_ref[...] = _dot(hn_ref[...], w_ref[...]).astype(big_ref.dtype)

    def dn_tile(kind):
        rb = span
        nrb = seq // rb
        rid = lax.broadcasted_iota(jnp.int32, (8, slab), 0)
        first = rid == 0
        last = rid == 7
        zero_row = jnp.zeros((1, slab), F32)
        for s in range(tn // slab):
            cols = slice(s * slab, (s + 1) * slab)
            cw = cw_ref[:, cols]
            prev_last = zero_row
            cur = _dot(hn_ref[0:rb, :], w_ref[:, cols])
            for r in range(nrb):
                if r + 1 < nrb:
                    nxt = _dot(hn_ref[(r + 1) * rb:(r + 2) * rb, :], w_ref[:, cols])
                    next_first = nxt[0:1]
                else:
                    nxt = None
                    next_first = zero_row
                xp = pltpu.roll(cur, 1, 0)
                xp = jnp.concatenate([jnp.where(first, prev_last, xp[0:8]), xp[8:]], axis=0)
                xn = pltpu.roll(cur, rb - 1, 0)
                xn = jnp.concatenate([xn[:rb - 8], jnp.where(last, next_first, xn[rb - 8:])], axis=0)
                y = cw[0:1] * xp + cw[1:2] * cur + cw[2:3] * xn
                y = y * _sigmoid(y)
                if kind == 0:
                    y = _l2norm_heads(y, DN_DK ** -0.5)
                elif kind == 1:
                    y = _l2norm_heads(y, None)
                big_ref[r * rb:(r + 1) * rb, cols] = y.astype(big_ref.dtype)
                prev_last = cur[rb - 1:rb]
                cur = nxt

    for kind in range(3):
        pl.when(jnp.logical_and(j >= dn0 + kind * per_group, j < dn0 + (kind + 1) * per_group))(
            functools.partial(dn_tile, kind))


def _in_proj(x2, g, w_big, wg_t, conv_w, *, layer, seq, tn, span):
    m = x2.shape[0]
    n = w_big.shape[-1]
    tm = seq
    dn0 = 4 * D_MODEL // tn
    n_dn = 3 * D_MODEL // tn
    return pl.pallas_call(
        functools.partial(_in_proj_kernel, span=span, seq=seq, tn=tn),
        grid=(m // tm, n // tn),
        in_specs=[
            pl.BlockSpec((tm, D_MODEL), lambda i, j: (i, 0)),
            pl.BlockSpec((1, D_MODEL), lambda i, j: (0, 0)),
            pl.BlockSpec((None, D_MODEL, tn), lambda i, j: (layer, 0, j)),
            pl.BlockSpec((None, GATE_ROWS, D_MODEL), lambda i, j: (layer, 0, 0)),
            pl.BlockSpec((8, tn), lambda i, j: (0, jnp.clip(j - dn0, 0, n_dn - 1))),
        ],
        out_specs=[
            pl.BlockSpec((tm, tn), lambda i, j: (i, j)),
            pl.BlockSpec((tm // span, GATE_ROWS, span), lambda i, j: (i, 0, 0)),
        ],
        out_shape=[
            jax.ShapeDtypeStruct((m, n), BF16),
            jax.ShapeDtypeStruct((m // span, GATE_ROWS, span), F32),
        ],
        scratch_shapes=[pltpu.VMEM((tm, D_MODEL), BF16)],
        compiler_params=pltpu.CompilerParams(
            dimension_semantics=("arbitrary", "arbitrary"), vmem_limit_bytes=VMEM_LIMIT),
        name="in_proj",
    )(x2, g, w_big, wg_t, conv_w)


def _mlstm_kernel(q_ref, k_ref, v_ref, o_ref, gt_ref, gb_ref, ng_ref, y_ref,
                  c_ref, hm_ref, rep_ref, rrow_ref, *, seq, span, hb):
    nc = seq // span
    scale = M_DK ** -0.5
    rw = REP_LANES
    chains = [(hh, fwd) for hh in range(hb) for fwd in (True, False)]
    nch = len(chains)
    each = lambda f, *ls: [f(*a) for a in zip(*ls)]
    wide = lambda t: jnp.concatenate([t] * (span // rw), axis=1)

    c_ref[...] = jnp.zeros_like(c_ref)
    hm_ref[...] = jnp.zeros_like(hm_ref)

    def gates(c, carry):
        ri = lax.broadcasted_iota(jnp.int32, (span, span), 0)
        ci = lax.broadcasted_iota(jnp.int32, (span, span), 1)
        eye_bf = (ri == ci).astype(BF16)
        cum = {True: (ri <= ci).astype(BF16), False: (ri >= ci).astype(BF16)}
        lane = lax.broadcasted_iota(jnp.int32, (8, span), 1)
        ek = lax.broadcasted_iota(jnp.int32, (rw, 3 * rw), 0)
        el = lax.broadcasted_iota(jnp.int32, (rw, 3 * rw), 1)
        pick_bf = jnp.logical_and(ek < 24, (ek & 7) == (el >> 7)).astype(BF16)
        zero7 = jnp.zeros((7, span), F32)
        zero6 = jnp.zeros((6, span), F32)
        zero5 = jnp.zeros((5, span), F32)
        work = [(n, hh, f, c * gpi + j) for j in range(gpi) for n, (hh, f) in enumerate(chains)]
        g8 = [gt_ref[x, hh] + gb_ref[hh] for _, hh, _, x in work]
        ig = [g[0:1] if f else g[2:3] for g, (_, _, f, _) in zip(g8, work)]
        lf = [-_softplus(-(g[1:2] if f else g[3:4])) for g, (_, _, f, _) in zip(g8, work)]
        b = [_rows_times_mask(jnp.concatenate([l, zero7], axis=0), cum[f])[0:1]
             for l, (_, _, f, _) in zip(lf, work)]
        r = each(lambda i, bb: i - bb, ig, b)
        mx = [_scan_max(jnp.broadcast_to(x, (8, span)), lane, f)[0:1] for x, (_, _, f, _) in zip(r, work)]
        pieces = [_split3_rows(jnp.concatenate([bb, i, m, zero5], axis=0), pad_to=rw)
                  for bb, i, m in zip(b, ig, mx)]
        cols = [_dot_nt(eye_bf, p) for p in pieces]
        rep = [_dot(t.astype(BF16), pick_bf) for t in cols]
        for w, (n, _, _, x) in enumerate(work):
            rep_ref[n, x] = rep[w]
            rrow_ref[n, x] = jnp.concatenate([r[w], lf[w], zero6], axis=0)
        return carry

    gpi = 4 if nc % 4 == 0 else 1
    lax.fori_loop(0, nc // gpi, gates, 0)

    def body(c, carry):
        n_st = list(carry[:nch])
        m_st = list(carry[nch:])
        ri = lax.broadcasted_iota(jnp.int32, (span, span), 0)
        ci = lax.broadcasted_iota(jnp.int32, (span, span), 1)
        mask = {True: ci <= ri, False: ci >= ri}
        ones_bf = jnp.ones((span, rw), BF16)
        cc = [c if f else nc - 1 - c for _, f in chains]
        rows = [pl.ds(pl.multiple_of(x * span, span), span) for x in cc]
        hcols = [slice(hh * M_DK, (hh + 1) * M_DK) for hh, _ in chains]
        rep = [rep_ref[n, x] for n, x in enumerate(cc)]
        rr = [rrow_ref[n, x] for n, x in enumerate(cc)]
        b_rep = [t[:, 0:rw] for t in rep]
        ig_rep = [t[:, rw:2 * rw] for t in rep]
        mx_rep = [t[:, 2 * rw:3 * rw] for t in rep]
        r_row = [t[0:1] for t in rr]
        b_last = [jnp.sum(t[1:2], axis=1, keepdims=True) for t in rr]
        q = [(q_ref[rw_, cs].astype(F32) * scale).astype(BF16) for rw_, cs in zip(rows, hcols)]
        k = [k_ref[rw_, cs] for rw_, cs in zip(rows, hcols)]
        v = [v_ref[rw_, cs] for rw_, cs in zip(rows, hcols)]
        s = each(_dot_nt, q, k)
        c_st = [c_ref[n] for n in range(nch)]
        qc = each(lambda a, t: _dot(a, t.astype(BF16)), q, c_st)
        qn = each(lambda a, t: _dot_nt(a, jnp.broadcast_to(t, (rw, M_DK)).astype(BF16)), q, n_st)
        c1 = each(lambda m, t: -jnp.maximum(m, t), m_st, mx_rep)
        inter = each(lambda m, t: jnp.exp(m + t), m_st, c1)
        e_negm = each(lambda t, bb: jnp.exp(t - bb), c1, b_rep)
        d = [jnp.exp(jnp.where(mask[f], rr_ + wide(t), NEG)) for rr_, t, (_, f) in zip(r_row, c1, chains)]
        sc = each(lambda a, t: (a * t).astype(BF16), s, d)
        scv = each(_dot, sc, v)
        rs = [_dot(t, ones_bf) for t in sc]
        den = each(lambda i, a, t: i * a + t, inter, qn, rs)
        inv = each(lambda t, e: 1.0 / jnp.maximum(jnp.abs(t), e), den, e_negm)
        for n, (hh, _) in enumerate(chains):
            hm_ref[hh, rows[n], :] += wide(inter[n] * inv[n]) * qc[n] + wide(inv[n]) * scv[n]

        a_rep = each(lambda bl, bb, i: bl - bb + i, b_last, b_rep, ig_rep)
        m_new = each(lambda bl, m, a: jnp.maximum(
            bl + m, jnp.max(jnp.max(a, axis=0, keepdims=True), axis=1, keepdims=True)),
            b_last, m_st, a_rep)
        wgt = each(lambda a, m: jnp.exp(a - m), a_rep, m_new)
        dec = each(lambda bl, m, mn: jnp.exp(bl + m - mn), b_last, m_st, m_new)
        kw = each(lambda a, w: a.astype(F32) * wide(w), k, wgt)
        upd = each(lambda a, t: _dot_tn(a.astype(BF16), t), kw, v)
        for n in range(nch):
            c_ref[n] = dec[n] * c_st[n] + upd[n]
        n_new = each(lambda dd, t, a: dd * t + jnp.sum(a, axis=0, keepdims=True), dec, n_st, kw)
        return tuple(n_new) + tuple(m_new)

    n0 = jnp.zeros((1, M_DK), F32)
    m0 = jnp.zeros((1, 1), F32)
    lax.fori_loop(0, nc, body, (n0,) * nch + (m0,) * nch)

    def finish(t, carry):
        rows = pl.ds(pl.multiple_of(t * span, span), span)
        for hh in range(hb):
            cols_h = slice(hh * M_DK, (hh + 1) * M_DK)
            h = hm_ref[hh, rows, :]
            y = h * lax.rsqrt(jnp.mean(h * h, axis=-1, keepdims=True) + EPS) * ng_ref[hh]
            y_ref[rows, cols_h] = (_sigmoid(o_ref[rows, cols_h].astype(F32)) * y).astype(y_ref.dtype)
        return carry

    lax.fori_loop(0, nc, finish, 0)


def _mlstm(big3, gt5, gate_b, norm_g, *, span, hb):
    b, s, _ = big3.shape
    nc = s // span
    wb = hb * M_DK
    groups = M_HEADS // hb
    q_spec = lambda off: pl.BlockSpec((None, s, wb), lambda i, h, off=off: (i, 0, off + h))
    return pl.pallas_call(
        functools.partial(_mlstm_kernel, seq=s, span=span, hb=hb),
        grid=(b, groups),
        in_specs=[
            q_spec(0), q_spec(groups), q_spec(2 * groups), q_spec(3 * groups),
            pl.BlockSpec((None, nc, hb, 8, span), lambda i, h: (i, 0, h, 0, 0)),
            pl.BlockSpec((hb, 8, span), lambda i, h: (h, 0, 0)),
            pl.BlockSpec((hb, 1, M_DK), lambda i, h: (h, 0, 0)),
        ],
        out_specs=pl.BlockSpec((None, s, wb), lambda i, h: (i, 0, h)),
        out_shape=jax.ShapeDtypeStruct((b, s, D_MODEL), BF16),
        scratch_shapes=[
            pltpu.VMEM((2 * hb, M_DK, M_DK), F32),
            pltpu.VMEM((hb, s, M_DK), F32),
            pltpu.VMEM((2 * hb, nc, span, 3 * REP_LANES), F32),
            pltpu.VMEM((2 * hb, nc, 8, span), F32),
        ],
        compiler_params=pltpu.CompilerParams(
            dimension_semantics=("arbitrary", "arbitrary"), vmem_limit_bytes=VMEM_LIMIT),
        name="mlstm",
    )(big3, big3, big3, big3, gt5, gate_b, norm_g)


def _log2(n):
    return n.bit_length() - 1


def _deltanet_kernel(q_ref, k_ref, v_ref, z_ref, gt_ref, p_ref,
                     ng_ref, y_ref, pq_ref, nn_ref, gl_ref, acc_ref,
                     *, seq, span, hb):
    nsp = seq // span
    lc = DN_CHUNK
    nck = seq // lc
    per = span // lc
    sh = _log2(lc)
    spi = 4

    def fold(full):
        lane_grp = lax.broadcasted_iota(jnp.int32, (lc, span), 1) >> sh
        out = full[(per - 1) * lc:per * lc]
        for c in range(per - 2, -1, -1):
            out = jnp.where(lane_grp == c, full[c * lc:(c + 1) * lc], out)
        return out

    def widen(col):
        lane_grp = lax.broadcasted_iota(jnp.int32, (lc, span), 1) >> sh
        out = jnp.broadcast_to(col[(per - 1) * lc:per * lc], (lc, span))
        for c in range(per - 2, -1, -1):
            out = jnp.where(lane_grp == c, jnp.broadcast_to(col[c * lc:(c + 1) * lc], (lc, span)), out)
        return out

    def blockdiag(wide_bf):
        ri = lax.broadcasted_iota(jnp.int32, (span, span), 0)
        ci = lax.broadcasted_iota(jnp.int32, (span, span), 1)
        tiled = jnp.concatenate([wide_bf] * per, axis=0)
        return jnp.where((ri >> sh) == (ci >> sh), tiled, jnp.zeros_like(tiled))

    def wide_inverse(aws, ri, cj):
        eye = (ri == cj).astype(F32)
        same16 = (ri >> 4) == (cj >> 4)
        ns = [jnp.where(same16, aw, 0.0).astype(BF16) for aw in aws]
        xs = [eye - n.astype(F32) for n in ns]
        ns = [_dot(n, blockdiag(n)).astype(BF16) for n in ns]
        for _ in range(2):
            rs = [_dot(jnp.concatenate([x.astype(BF16), n], axis=0), blockdiag(n)) for x, n in zip(xs, ns)]
            xs = [x + r[:lc] for x, r in zip(xs, rs)]
            ns = [r[lc:].astype(BF16) for r in rs]
        xs = [x + _dot(x.astype(BF16), blockdiag(n)) for x, n in zip(xs, ns)]
        lvl = 4
        while (1 << lvl) < lc:
            inner = (ri >> lvl) == (cj >> lvl)
            outer = (ri >> (lvl + 1)) == (cj >> (lvl + 1))
            pick = jnp.logical_and(outer, jnp.logical_not(inner))
            xb = [x.astype(BF16) for x in xs]
            xo = [_dot(x, blockdiag(jnp.where(pick, aw, 0.0).astype(BF16))).astype(BF16)
                  for x, aw in zip(xb, aws)]
            xs = [x - _dot(t, blockdiag(b)) for x, t, b in zip(xs, xo, xb)]
            lvl += 1
        return xs

    def prep(it, carry):
        ri_f = lax.broadcasted_iota(jnp.int32, (span, span), 0)
        ci_f = lax.broadcasted_iota(jnp.int32, (span, span), 1)
        same_f = (ri_f >> sh) == (ci_f >> sh)
        eye_bf = (ri_f == ci_f).astype(BF16)
        cum_bf = jnp.concatenate([jnp.logical_and(same_f, ri_f <= ci_f).astype(BF16),
                                  jnp.logical_and(same_f, ri_f >= ci_f).astype(BF16)], axis=1)
        ri = lax.broadcasted_iota(jnp.int32, (lc, span), 0)
        cj = lax.broadcasted_iota(jnp.int32, (lc, span), 1) & (lc - 1)
        zero6 = jnp.zeros((6, span), F32)
        zero2 = jnp.zeros((2, span), F32)

        units = [(hh, it * spi + j) for j in range(spi) for hh in range(hb)]
        rows_u = [pl.ds(pl.multiple_of(s * span, span), span) for _, s in units]
        hcols = [slice(hh * DN_DK, (hh + 1) * DN_DK) for hh, _ in units]
        k_bf = [k_ref[r, cs] for r, cs in zip(rows_u, hcols)]
        q_bf = [q_ref[r, cs] for r, cs in zip(rows_u, hcols)]
        kf = [t.astype(F32) for t in k_bf]
        qf = [t.astype(F32) for t in q_bf]
        vf = [v_ref[r, cs].astype(F32) for r, cs in zip(rows_u, hcols)]
        kq = [_dot_nt(jnp.concatenate([k, q], axis=0), k) for k, q in zip(k_bf, q_bf)]
        kkw = [fold(t[:span]) for t in kq]
        qkw = [fold(t[span:]) for t in kq]

        g8 = [gt_ref[s, hh] for hh, s in units]
        pr = [p_ref[hh] for hh, _ in units]
        beta_f = [_sigmoid(g[0:1]) for g in g8]
        beta_b = [_sigmoid(g[2:3]) for g in g8]
        g_f = [-jnp.exp(p[0:1]) * _softplus(g[1:2] + p[2:3]) for g, p in zip(g8, pr)]
        g_b = [-jnp.exp(p[1:2]) * _softplus(g[3:4] + p[3:4]) for g, p in zip(g8, pr)]
        g2 = [jnp.concatenate([a, b, zero6], axis=0) for a, b in zip(g_f, g_b)]
        cum = [_rows_times_mask(g, cum_bf) for g in g2]
        cum_fwd = [c[:, :span] for c in cum]
        cum_rev = [c[:, span:] for c in cum]
        tot = [a + b - g for a, b, g in zip(cum_fwd, cum_rev, g2)]
        cols = [_rows_to_cols(jnp.concatenate([bf, cf[0:1], t[0:1], bb, cr[1:2], t[1:2], zero2], axis=0),
                              eye_bf)
                for bf, cf, t, bb, cr in zip(beta_f, cum_fwd, tot, beta_b, cum_rev)]

        chains = [(u, fwd) for u in range(len(units)) for fwd in (True, False)]
        col = lambda u, fwd, k: cols[u][:, (0 if fwd else 3) + k:(0 if fwd else 3) + k + 1]
        beta_col = [col(u, f, 0) for u, f in chains]
        gc_col = [col(u, f, 1) for u, f in chains]
        tot_col = [col(u, f, 2) for u, f in chains]
        gc_row = [cum_fwd[u][0:1] if f else cum_rev[u][1:2] for u, f in chains]
        tot_row = [tot[u][0:1] if f else tot[u][1:2] for u, f in chains]
        decay = [jnp.where((cj <= ri) if f else (cj >= ri), jnp.exp(widen(gc) - gr), 0.0)
                 for (u, f), gc, gr in zip(chains, gc_col, gc_row)]
        attn = [qkw[u] * d for (u, f), d in zip(chains, decay)]
        aws = [jnp.where((cj < ri) if f else (cj > ri), kkw[u] * d * widen(bc), 0.0)
               for (u, f), d, bc in zip(chains, decay, beta_col)]
        tws = wide_inverse(aws, ri, cj)
        e_gc = [jnp.exp(gc) for gc in gc_col]
        rhs = [jnp.concatenate([(kf[u] * (bc * e)).astype(BF16), (vf[u] * bc).astype(BF16)], axis=1)
               for (u, f), bc, e in zip(chains, beta_col, e_gc)]
        wu_bf = [_dot(blockdiag(t.astype(BF16)), r).astype(BF16) for t, r in zip(tws, rhs)]
        awu = [_dot(blockdiag(a.astype(BF16)), w) for a, w in zip(attn, wu_bf)]
        qp = [(qf[u] * e - t[:, :DN_DK]).astype(BF16) for (u, f), e, t in zip(chains, e_gc, awu)]
        kd = [(kf[u] * jnp.exp(tc - gc)).astype(BF16) for (u, f), tc, gc in zip(chains, tot_col, gc_col)]
        gl_row = [jnp.exp(t) for t in tot_row]
        for n, (u, f) in enumerate(chains):
            hh, s = units[u]
            idx = 2 * hh + (0 if f else 1)
            for c in range(per):
                cr = slice(c * lc, (c + 1) * lc)
                pn = _dot_tn(kd[n][cr], wu_bf[n][cr])
                chunk = s * per + c
                pq_ref[idx, chunk, 0:DN_DK, :] = pn[:, :DN_DK].astype(BF16)
                pq_ref[idx, chunk, DN_DK:DN_DK + lc, :] = qp[n][cr]
                nn_ref[idx, chunk] = pn[:, DN_DK:]
                gl_ref[idx, chunk] = jnp.broadcast_to(gl_row[n][:, c * lc:c * lc + 1], (1, DN_DK))
        for u, (hh, s) in enumerate(units):
            acc_ref[hh, rows_u[u], :] = awu[2 * u][:, DN_DK:] + awu[2 * u + 1][:, DN_DK:]
        return carry

    lax.fori_loop(0, nsp // spi, prep, 0)

    def step(c, st, hh, idx):
        r = _dot(pq_ref[idx, c], st.astype(BF16))
        rows = pl.ds(pl.multiple_of(c * lc, lc), lc)
        acc_ref[hh, rows, :] += r[DN_DK:]
        return gl_ref[idx, c] * st - r[:DN_DK] + nn_ref[idx, c]

    def body(c, carry):
        out = []
        for hh in range(hb):
            out.append(step(c, carry[2 * hh], hh, 2 * hh))
            out.append(step(nck - 1 - c, carry[2 * hh + 1], hh, 2 * hh + 1))
        return tuple(out)

    s0 = jnp.zeros((DN_DK, DN_DK), F32)
    lax.fori_loop(0, nck, body, (s0,) * (2 * hb))

    def finish(t, carry):
        rows = pl.ds(pl.multiple_of(t * span, span), span)
        for hh in range(hb):
            cols_h = slice(hh * DN_DK, (hh + 1) * DN_DK)
            o = acc_ref[hh, rows, :]
            y = o * lax.rsqrt(jnp.mean(o * o, axis=-1, keepdims=True) + EPS) * ng_ref[...]
            z = z_ref[rows, cols_h].astype(F32)
            y_ref[rows, cols_h] = (y * (z * _sigmoid(z))).astype(y_ref.dtype)
        return carry

    lax.fori_loop(0, nsp, finish, 0)


def _deltanet(big3, gt5, params, norm_g, *, span, hb):
    b, s, _ = big3.shape
    nsp = s // span
    nck = s // DN_CHUNK
    wb = hb * DN_DK
    groups = DN_HEADS // hb
    base = 4 * D_MODEL // wb
    slot0 = M_HEADS // hb
    q_spec = lambda off: pl.BlockSpec((None, s, wb), lambda i, h, off=off: (i, 0, base + off + h))
    return pl.pallas_call(
        functools.partial(_deltanet_kernel, seq=s, span=span, hb=hb),
        grid=(b, groups),
        in_specs=[
            q_spec(0), q_spec(groups), q_spec(2 * groups), q_spec(3 * groups),
            pl.BlockSpec((None, nsp, hb, 8, span), lambda i, h: (i, 0, slot0 + h, 0, 0)),
            pl.BlockSpec((hb, 8, span), lambda i, h: (h, 0, 0)),
            pl.BlockSpec((1, DN_DK), lambda i, h: (0, 0)),
        ],
        out_specs=pl.BlockSpec((None, s, wb), lambda i, h: (i, 0, h)),
        out_shape=jax.ShapeDtypeStruct((b, s, D_MODEL), BF16),
        scratch_shapes=[
            pltpu.VMEM((2 * hb, nck, DN_DK + DN_CHUNK, DN_DK), BF16),
            pltpu.VMEM((2 * hb, nck, DN_DK, DN_DK), F32),
            pltpu.VMEM((2 * hb, nck, 1, DN_DK), F32),
            pltpu.VMEM((hb, s, DN_DK), F32),
        ],
        compiler_params=pltpu.CompilerParams(
            dimension_semantics=("arbitrary", "arbitrary"), vmem_limit_bytes=VMEM_LIMIT),
        name="deltanet",
    )(big3, big3, big3, big3, gt5, params, norm_g)


HALO = 16


def _merge_kernel(x_ref, ym_ref, yd_ref, sb_ref, sc_ref, sx_ref, cp_ref, xp_ref, cn_ref, xn_ref,
                  cw_ref, p0_ref, p1_ref, p2_ref, wb_ref, wo_ref, o_ref, *, tiles_per_seq):
    i = pl.program_id(0)
    tm = x_ref.shape[0]
    cx = sc_ref[...].astype(F32) * sx_ref[...].astype(F32)
    has_prev = jnp.where(i % tiles_per_seq != 0, 1.0, 0.0)
    has_next = jnp.where(i % tiles_per_seq != tiles_per_seq - 1, 1.0, 0.0)
    prev_row = cp_ref[HALO - 1:HALO, :].astype(F32) * xp_ref[HALO - 1:HALO, :].astype(F32) * has_prev
    next_row = cn_ref[0:1, :].astype(F32) * xn_ref[0:1, :].astype(F32) * has_next
    rid = lax.broadcasted_iota(jnp.int32, (8, D_MODEL), 0)
    cp = pltpu.roll(cx, 1, 0)
    cp = jnp.concatenate([jnp.where(rid == 0, prev_row, cp[0:8]), cp[8:]], axis=0)
    cn = pltpu.roll(cx, tm - 1, 0)
    cn = jnp.concatenate([cn[:tm - 8], jnp.where(rid == 7, next_row, cn[tm - 8:])], axis=0)
    conv = cw_ref[0:1, :] * cp + cw_ref[1:2, :] * cx + cw_ref[2:3, :] * cn
    yc = (sb_ref[...].astype(F32) * conv).astype(BF16)

    mixed = _sigmoid(p0_ref[...].astype(F32)) * _dot(ym_ref[...], wb_ref[0])
    mixed += _sigmoid(p1_ref[...].astype(F32)) * _dot(yd_ref[...], wb_ref[1])
    mixed += _sigmoid(p2_ref[...].astype(F32)) * _dot(yc, wb_ref[2])
    o_ref[...] = x_ref[...] + _dot(mixed.astype(BF16), wo_ref[...])


def _merge(x2, ym, yd, big2, conv_w, w_branch, w_out, *, layer, seq, tm):
    m = x2.shape[0]
    hpt = tm // HALO
    last = m // HALO - 1
    row = lambda: pl.BlockSpec((tm, D_MODEL), lambda i: (i, 0))
    col = lambda c: pl.BlockSpec((tm, D_MODEL), lambda i, c=c: (i, c))
    prev = lambda c: pl.BlockSpec((HALO, D_MODEL), lambda i, c=c: (jnp.maximum(i * hpt - 1, 0), c))
    nxt = lambda c: pl.BlockSpec((HALO, D_MODEL), lambda i, c=c: (jnp.minimum((i + 1) * hpt, last), c))
    sc_b, sc_c, sc_x, pre0 = 8, 9, 10, 11
    return pl.pallas_call(
        functools.partial(_merge_kernel, tiles_per_seq=seq // tm),
        grid=(m // tm,),
        in_specs=[row(), row(), row(), col(sc_b), col(sc_c), col(sc_x),
                  prev(sc_c), prev(sc_x), nxt(sc_c), nxt(sc_x),
                  pl.BlockSpec((8, D_MODEL), lambda i: (0, 0)),
                  col(pre0), col(pre0 + 1), col(pre0 + 2),
                  pl.BlockSpec((None, 3, D_MODEL, D_MODEL), lambda i: (layer, 0, 0, 0)),
                  pl.BlockSpec((None, D_MODEL, D_MODEL), lambda i: (layer, 0, 0))],
        out_specs=row(),
        out_shape=jax.ShapeDtypeStruct((m, D_MODEL), F32),
        compiler_params=pltpu.CompilerParams(
            dimension_semantics=("arbitrary",), vmem_limit_bytes=VMEM_LIMIT),
        name="merge",
    )(x2, ym, yd, big2, big2, big2, big2, big2, big2, big2, conv_w,
      big2, big2, big2, w_branch, w_out)


def _mlp_kernel(x_ref, g_ref, wu_ref, wd_ref, gf_ref, o_ref, *, ff_chunk, final_norm):
    x = x_ref[...]
    h = (x * lax.rsqrt(jnp.mean(x * x, axis=-1, keepdims=True) + EPS) * g_ref[...]).astype(BF16)
    out = x
    for c in range(D_FF // ff_chunk):
        cols = slice(c * ff_chunk, (c + 1) * ff_chunk)
        up = jnp.maximum(_dot(h, wu_ref[:, cols]), 0.0)
        out = out + _dot((up * up).astype(BF16), wd_ref[cols, :])
    if final_norm:
        out = out * lax.rsqrt(jnp.mean(out * out, axis=-1, keepdims=True) + EPS) * gf_ref[...]
    o_ref[...] = out


def _mlp(x2, g, w_up, w_down, g_final, *, layer, tm, final_norm):
    m = x2.shape[0]
    return pl.pallas_call(
        functools.partial(_mlp_kernel, ff_chunk=1024, final_norm=final_norm),
        grid=(m // tm,),
        in_specs=[
            pl.BlockSpec((tm, D_MODEL), lambda i: (i, 0)),
            pl.BlockSpec((1, D_MODEL), lambda i: (0, 0)),
            pl.BlockSpec((None, D_MODEL, D_FF), lambda i: (layer, 0, 0)),
            pl.BlockSpec((None, D_FF, D_MODEL), lambda i: (layer, 0, 0)),
            pl.BlockSpec((1, D_MODEL), lambda i: (0, 0)),
        ],
        out_specs=pl.BlockSpec((tm, D_MODEL), lambda i: (i, 0)),
        out_shape=jax.ShapeDtypeStruct((m, D_MODEL), F32),
        compiler_params=pltpu.CompilerParams(
            dimension_semantics=("arbitrary",), vmem_limit_bytes=VMEM_LIMIT),
        name="mlp",
    )(x2, g, w_up, w_down, g_final)


def _prep_w_in(w_in):
    d = w_in.shape[0]
    c_mg = 4 * D_MODEL
    c_dn = c_mg + 4 * M_HEADS
    c_dg = c_dn + 4 * D_MODEL
    c_sc = c_dg + 4 * DN_HEADS
    w_big = jnp.concatenate([w_in[:, :c_mg], w_in[:, c_dn:c_dg], w_in[:, c_sc:]], axis=1)
    mg = w_in[:, c_mg:c_dn].reshape(d, 4, M_HEADS).transpose(2, 1, 0)
    dg = w_in[:, c_dg:c_sc].reshape(d, 4, DN_HEADS).transpose(2, 1, 0)
    pad = lambda t: jnp.pad(t, ((0, 0), (0, 4), (0, 0))).reshape(-1, d)
    wg_t = jnp.concatenate([pad(mg), pad(dg)], axis=0)
    wg_t = jnp.pad(wg_t, ((0, GATE_ROWS - wg_t.shape[0]), (0, 0)))
    return w_big.astype(BF16), wg_t.astype(BF16)


def _rows8(vals, width):
    h, r = vals.shape
    t = jnp.pad(vals.astype(F32), ((0, 0), (0, 8 - r)))
    return jnp.broadcast_to(t[:, :, None], (h, 8, width))


def _layer(x2, batch, seq, layer, p, big_w, g_final, final_norm):
    m = x2.shape[0]
    span = SPAN
    hb = HEADS_PER_STEP
    dn_conv = jnp.pad(p["dn_conv_w"].astype(F32), ((0, 5), (0, 0)))
    big, gt = _in_proj(x2, p["norm_mix_g"].reshape(1, -1), big_w["w_big"], big_w["wg_t"], dn_conv,
                       layer=layer, seq=seq, tn=1024, span=span)
    big3 = big.reshape(batch, seq, N_BIG)
    gt5 = gt.reshape(batch, seq // span, GATE_ROWS // 8, 8, span)

    gate_b = _rows8(p["m_gate_b"].T, span)
    ym = _mlstm(big3, gt5, gate_b, p["m_norm_g"].reshape(M_HEADS, 1, M_DK), span=span, hb=hb)

    dn_params = _rows8(jnp.concatenate([p["dn_a_log"].T, p["dn_dt_bias"].T], axis=1), span)
    yd = _deltanet(big3, gt5, dn_params, p["dn_norm_g"].reshape(1, -1), span=span, hb=hb)

    sc_w = jnp.pad(p["sc_conv_w"].astype(F32), ((0, 5), (0, 0)))
    x2 = _merge(x2, ym.reshape(m, -1), yd.reshape(m, -1), big, sc_w,
                big_w["w_branch"], big_w["w_out"], layer=layer, seq=seq, tm=512)
    x2 = _mlp(x2, p["norm_mlp_g"].reshape(1, -1), big_w["w_up"], big_w["w_down"],
              g_final.reshape(1, -1), layer=layer, tm=512, final_norm=final_norm)
    return x2


def kernel(x, norm_mix_g, w_in, m_gate_b, m_norm_g, dn_conv_w, dn_a_log, dn_dt_bias,
           dn_norm_g, sc_conv_w, w_branch, w_out, norm_mlp_g, w_up, w_down, norm_final_g):
    batch, seq, d = x.shape
    depth = w_in.shape[0]
    x2 = x.reshape(batch * seq, d)
    w_big, wg_t = jax.vmap(_prep_w_in)(w_in)
    big_w = dict(w_big=w_big, wg_t=wg_t, w_branch=w_branch.astype(BF16), w_out=w_out.astype(BF16),
                 w_up=w_up.astype(BF16), w_down=w_down.astype(BF16))
    small = dict(norm_mix_g=norm_mix_g, m_gate_b=m_gate_b, m_norm_g=m_norm_g,
                 dn_conv_w=dn_conv_w, dn_a_log=dn_a_log, dn_dt_bias=dn_dt_bias,
                 dn_norm_g=dn_norm_g, sc_conv_w=sc_conv_w, norm_mlp_g=norm_mlp_g)
    for l in range(depth):
        p = {k: v[l] for k, v in small.items()}
        x2 = _layer(x2, batch, seq, l, p, big_w, norm_final_g, final_norm=(l == depth - 1))
    return x2.reshape(batch, seq, d)
```

```python
import functools

import jax
import jax.numpy as jnp
from jax import lax
from jax.experimental import pallas as pl
from jax.experimental.pallas import tpu as pltpu

F32 = jnp.float32
BF16 = jnp.bfloat16

D_MODEL = 1024
M_HEADS = 4
M_DK = 256
DN_HEADS = 8
DN_DK = 128
DN_CHUNK = 64
D_FF = 4096
EPS = 1e-6

SPAN = 256
GATE_ROWS = 128
N_BIG = 14 * D_MODEL
NEG = -1e30
HEADS_PER_STEP = 2
REP_LANES = 128

VMEM_LIMIT = 56 * 1024 * 1024


def _dot(a, b):
    return jnp.dot(a, b, preferred_element_type=F32)


def _dot_nt(a, b):
    return lax.dot_general(a, b, (((1,), (1,)), ((), ())), preferred_element_type=F32)


def _dot_tn(a, b):
    return lax.dot_general(a, b, (((0,), (0,)), ((), ())), preferred_element_type=F32)


def _split3_rows(a, pad_to=None):
    a1 = a.astype(BF16).astype(F32)
    r1 = a - a1
    a2 = r1.astype(BF16).astype(F32)
    a3 = r1 - a2
    parts = [a1, a2, a3]
    if pad_to is not None:
        parts.append(jnp.zeros((pad_to - 3 * a.shape[0], a.shape[1]), F32))
    return jnp.concatenate(parts, axis=0).astype(BF16)


def _scan_max(x, lane, forward):
    n = x.shape[1]
    sh = 1
    while sh < n:
        if forward:
            x = jnp.where(lane >= sh, jnp.maximum(x, pltpu.roll(x, sh, 1)), x)
        else:
            x = jnp.where(lane < n - sh, jnp.maximum(x, pltpu.roll(x, n - sh, 1)), x)
        sh *= 2
    return x


def _rows_times_mask(rows, mask_bf):
    p = _dot(_split3_rows(rows), mask_bf)
    return (p[0:8] + p[8:16]) + p[16:24]


def _rows_to_cols(rows, eye_bf):
    p = _dot_nt(eye_bf, _split3_rows(rows))
    return (p[:, 0:8] + p[:, 8:16]) + p[:, 16:24]


def _softplus(x):
    return jnp.maximum(x, 0.0) + jnp.log(1.0 + jnp.exp(-jnp.abs(x)))


NEG_LOG2E = -1.4426950408889634


def _sigmoid(x):
    return 1.0 / (1.0 + jnp.exp2(x * NEG_LOG2E))


def _l2norm_heads(y, scale):
    outs = []
    for h in range(y.shape[1] // DN_DK):
        t = y[:, h * DN_DK:(h + 1) * DN_DK]
        inv = lax.rsqrt(jnp.sum(t * t, axis=-1, keepdims=True) + EPS)
        outs.append(t * (inv if scale is None else inv * scale))
    return jnp.concatenate(outs, axis=1)


def _in_proj_kernel(x_ref, g_ref, w_ref, wg_ref, cw_ref, big_ref, gt_ref, hn_ref, *, span, seq, tn):
    j = pl.program_id(1)
    dn0 = 4 * D_MODEL // tn
    per_group = D_MODEL // tn
    slab = 2 * DN_DK

    @pl.when(j == 0)
    def _():
        x = x_ref[...]
        y = x * lax.rsqrt(jnp.mean(x * x, axis=-1, keepdims=True) + EPS) * g_ref[...]
        hn = y.astype(BF16)
        hn_ref[...] = hn
        gt = _dot_nt(wg_ref[...], hn)
        for c in range(gt_ref.shape[0]):
            gt_ref[c] = gt[:, c * span:(c + 1) * span]

    in_dn = jnp.logical_and(j >= dn0, j < dn0 + 3 * per_group)

    @pl.when(jnp.logical_not(in_dn))
    def _():
        big_ref[...] = _dot(hn_ref[...], w_ref[...]).astype(big_ref.dtype)

    def dn_tile(kind):
        rb = span
        nrb = seq // rb
        rid = lax.broadcasted_iota(jnp.int32, (8, slab), 0)
        first = rid == 0
        last = rid == 7
        zero_row = jnp.zeros((1, slab), F32)
        for s in range(tn // slab):
            cols = slice(s * slab, (s + 1) * slab)
            cw = cw_ref[:, cols]
            prev_last = zero_row
            cur = _dot(hn_ref[0:rb, :], w_ref[:, cols])
            for r in range(nrb):
                if r + 1 < nrb:
                    nxt = _dot(hn_ref[(r + 1) * rb:(r + 2) * rb, :], w_ref[:, cols])
                    next_first = nxt[0:1]
                else:
                    nxt = None
                    next_first = zero_row
                xp = pltpu.roll(cur, 1, 0)
                xp = jnp.concatenate([jnp.where(first, prev_last, xp[0:8]), xp[8:]], axis=0)
                xn = pltpu.roll(cur, rb - 1, 0)
                xn = jnp.concatenate([xn[:rb - 8], jnp.where(last, next_first, xn[rb - 8:])], axis=0)
                y = cw[0:1] * xp + cw[1:2] * cur + cw[2:3] * xn
                y = y * _sigmoid(y)
                if kind == 0:
                    y = _l2norm_heads(y, DN_DK ** -0.5)
                elif kind == 1:
                    y = _l2norm_heads(y, None)
                big_ref[r * rb:(r + 1) * rb, cols] = y.astype(big_ref.dtype)
                prev_last = cur[rb - 1:rb]
                cur = nxt

    for kind in range(3):
        pl.when(jnp.logical_and(j >= dn0 + kind * per_group, j < dn0 + (kind + 1) * per_group))(
            functools.partial(dn_tile, kind))


def _in_proj(x2, g, w_big, wg_t, conv_w, *, layer, seq, span):
    m = x2.shape[0]
    n_tiles, tn = w_big.shape[1], w_big.shape[3]
    tm = seq
    dn0 = 4 * D_MODEL // tn
    n_dn = 3 * D_MODEL // tn
    return pl.pallas_call(
        functools.partial(_in_proj_kernel, span=span, seq=seq, tn=tn),
        grid=(m // tm, n_tiles),
        in_specs=[
            pl.BlockSpec((tm, D_MODEL), lambda i, j: (i, 0)),
            pl.BlockSpec((1, D_MODEL), lambda i, j: (0, 0)),
            pl.BlockSpec((None, None, D_MODEL, tn), lambda i, j: (layer, j, 0, 0)),
            pl.BlockSpec((None, GATE_ROWS, D_MODEL), lambda i, j: (layer, 0, 0)),
            pl.BlockSpec((8, tn), lambda i, j: (0, jnp.clip(j - dn0, 0, n_dn - 1))),
        ],
        out_specs=[
            pl.BlockSpec((None, tm, tn), lambda i, j: (j, i, 0)),
            pl.BlockSpec((tm // span, GATE_ROWS, span), lambda i, j: (i, 0, 0)),
        ],
        out_shape=[
            jax.ShapeDtypeStruct((n_tiles, m, tn), BF16),
            jax.ShapeDtypeStruct((m // span, GATE_ROWS, span), F32),
        ],
        scratch_shapes=[pltpu.VMEM((tm, D_MODEL), BF16)],
        compiler_params=pltpu.CompilerParams(
            dimension_semantics=("arbitrary", "arbitrary"), vmem_limit_bytes=VMEM_LIMIT),
        name="in_proj",
    )(x2, g, w_big, wg_t, conv_w)


def _mlstm_kernel(q_ref, k_ref, v_ref, o_ref, gt_ref, gb_ref, ng_ref, y_ref,
                  c_ref, hm_ref, rep_ref, rrow_ref, *, seq, span, hb):
    nc = seq // span
    scale = M_DK ** -0.5
    rw = REP_LANES
    chains = [(hh, fwd) for hh in range(hb) for fwd in (True, False)]
    nch = len(chains)
    each = lambda f, *ls: [f(*a) for a in zip(*ls)]
    wide = lambda t: jnp.concatenate([t] * (span // rw), axis=1)

    c_ref[...] = jnp.zeros_like(c_ref)
    hm_ref[...] = jnp.zeros_like(hm_ref)

    def gates(c, carry):
        ri = lax.broadcasted_iota(jnp.int32, (span, span), 0)
        ci = lax.broadcasted_iota(jnp.int32, (span, span), 1)
        eye_bf = (ri == ci).astype(BF16)
        cum = {True: (ri <= ci).astype(BF16), False: (ri >= ci).astype(BF16)}
        lane = lax.broadcasted_iota(jnp.int32, (8, span), 1)
        ek = lax.broadcasted_iota(jnp.int32, (rw, 3 * rw), 0)
        el = lax.broadcasted_iota(jnp.int32, (rw, 3 * rw), 1)
        pick_bf = jnp.logical_and(ek < 24, (ek & 7) == (el >> 7)).astype(BF16)
        zero7 = jnp.zeros((7, span), F32)
        zero6 = jnp.zeros((6, span), F32)
        zero5 = jnp.zeros((5, span), F32)
        work = [(n, hh, f, c * gpi + j) for j in range(gpi) for n, (hh, f) in enumerate(chains)]
        g8 = [gt_ref[x, hh] + gb_ref[hh] for _, hh, _, x in work]
        ig = [g[0:1] if f else g[2:3] for g, (_, _, f, _) in zip(g8, work)]
        lf = [-_softplus(-(g[1:2] if f else g[3:4])) for g, (_, _, f, _) in zip(g8, work)]
        b = [_rows_times_mask(jnp.concatenate([l, zero7], axis=0), cum[f])[0:1]
             for l, (_, _, f, _) in zip(lf, work)]
        r = each(lambda i, bb: i - bb, ig, b)
        mx = [_scan_max(jnp.broadcast_to(x, (8, span)), lane, f)[0:1] for x, (_, _, f, _) in zip(r, work)]
        pieces = [_split3_rows(jnp.concatenate([bb, i, m, zero5], axis=0), pad_to=rw)
                  for bb, i, m in zip(b, ig, mx)]
        cols = [_dot_nt(eye_bf, p) for p in pieces]
        rep = [_dot(t.astype(BF16), pick_bf) for t in cols]
        for w, (n, _, _, x) in enumerate(work):
            rep_ref[n, x] = rep[w]
            rrow_ref[n, x] = jnp.concatenate([r[w], lf[w], zero6], axis=0)
        return carry

    gpi = 4 if nc % 4 == 0 else 1
    lax.fori_loop(0, nc // gpi, gates, 0)

    def body(c, carry):
        n_st = list(carry[:nch])
        m_st = list(carry[nch:])
        ri = lax.broadcasted_iota(jnp.int32, (span, span), 0)
        ci = lax.broadcasted_iota(jnp.int32, (span, span), 1)
        mask = {True: ci <= ri, False: ci >= ri}
        ones_bf = jnp.ones((span, rw), BF16)
        cc = [c if f else nc - 1 - c for _, f in chains]
        rows = [pl.ds(pl.multiple_of(x * span, span), span) for x in cc]
        hcols = [slice(hh * M_DK, (hh + 1) * M_DK) for hh, _ in chains]
        rep = [rep_ref[n, x] for n, x in enumerate(cc)]
        rr = [rrow_ref[n, x] for n, x in enumerate(cc)]
        b_rep = [t[:, 0:rw] for t in rep]
        ig_rep = [t[:, rw:2 * rw] for t in rep]
        mx_rep = [t[:, 2 * rw:3 * rw] for t in rep]
        r_row = [t[0:1] for t in rr]
        b_last = [jnp.sum(t[1:2], axis=1, keepdims=True) for t in rr]
        q = [(q_ref[rw_, cs].astype(F32) * scale).astype(BF16) for rw_, cs in zip(rows, hcols)]
        k = [k_ref[rw_, cs] for rw_, cs in zip(rows, hcols)]
        v = [v_ref[rw_, cs] for rw_, cs in zip(rows, hcols)]
        s = each(_dot_nt, q, k)
        c_st = [c_ref[n] for n in range(nch)]
        qc = each(lambda a, t: _dot(a, t.astype(BF16)), q, c_st)
        qn = each(lambda a, t: _dot_nt(a, jnp.broadcast_to(t, (rw, M_DK)).astype(BF16)), q, n_st)
        c1 = each(lambda m, t: -jnp.maximum(m, t), m_st, mx_rep)
        inter = each(lambda m, t: jnp.exp(m + t), m_st, c1)
        e_negm = each(lambda t, bb: jnp.exp(t - bb), c1, b_rep)
        d = [jnp.exp(jnp.where(mask[f], rr_ + wide(t), NEG)) for rr_, t, (_, f) in zip(r_row, c1, chains)]
        sc = each(lambda a, t: (a * t).astype(BF16), s, d)
        scv = each(_dot, sc, v)
        rs = [_dot(t, ones_bf) for t in sc]
        den = each(lambda i, a, t: i * a + t, inter, qn, rs)
        inv = each(lambda t, e: 1.0 / jnp.maximum(jnp.abs(t), e), den, e_negm)
        for n, (hh, _) in enumerate(chains):
            hm_ref[hh, rows[n], :] += wide(inter[n] * inv[n]) * qc[n] + wide(inv[n]) * scv[n]

        a_rep = each(lambda bl, bb, i: bl - bb + i, b_last, b_rep, ig_rep)
        m_new = each(lambda bl, m, a: jnp.maximum(
            bl + m, jnp.max(jnp.max(a, axis=0, keepdims=True), axis=1, keepdims=True)),
            b_last, m_st, a_rep)
        wgt = each(lambda a, m: jnp.exp(a - m), a_rep, m_new)
        dec = each(lambda bl, m, mn: jnp.exp(bl + m - mn), b_last, m_st, m_new)
        kw = each(lambda a, w: a.astype(F32) * wide(w), k, wgt)
        upd = each(lambda a, t: _dot_tn(a.astype(BF16), t), kw, v)
        for n in range(nch):
            c_ref[n] = dec[n] * c_st[n] + upd[n]
        n_new = each(lambda dd, t, a: dd * t + jnp.sum(a, axis=0, keepdims=True), dec, n_st, kw)
        return tuple(n_new) + tuple(m_new)

    n0 = jnp.zeros((1, M_DK), F32)
    m0 = jnp.zeros((1, 1), F32)
    lax.fori_loop(0, nc, body, (n0,) * nch + (m0,) * nch)

    def finish(t, carry):
        rows = pl.ds(pl.multiple_of(t * span, span), span)
        for hh in range(hb):
            cols_h = slice(hh * M_DK, (hh + 1) * M_DK)
            h = hm_ref[hh, rows, :]
            y = h * lax.rsqrt(jnp.mean(h * h, axis=-1, keepdims=True) + EPS) * ng_ref[hh]
            y_ref[rows, cols_h] = (_sigmoid(o_ref[rows, cols_h].astype(F32)) * y).astype(y_ref.dtype)
        return carry

    lax.fori_loop(0, nc, finish, 0)


def _mlstm(big4, gt5, gate_b, norm_g, *, span, hb):
    _, b, s, _ = big4.shape
    nc = s // span
    wb = hb * M_DK
    groups = M_HEADS // hb
    q_spec = lambda t: pl.BlockSpec((None, None, s, wb), lambda i, h, t=t: (t, i, 0, h))
    return pl.pallas_call(
        functools.partial(_mlstm_kernel, seq=s, span=span, hb=hb),
        grid=(b, groups),
        in_specs=[
            q_spec(0), q_spec(1), q_spec(2), q_spec(3),
            pl.BlockSpec((None, nc, hb, 8, span), lambda i, h: (i, 0, h, 0, 0)),
            pl.BlockSpec((hb, 8, span), lambda i, h: (h, 0, 0)),
            pl.BlockSpec((hb, 1, M_DK), lambda i, h: (h, 0, 0)),
        ],
        out_specs=pl.BlockSpec((None, s, wb), lambda i, h: (i, 0, h)),
        out_shape=jax.ShapeDtypeStruct((b, s, D_MODEL), BF16),
        scratch_shapes=[
            pltpu.VMEM((2 * hb, M_DK, M_DK), F32),
            pltpu.VMEM((hb, s, M_DK), F32),
            pltpu.VMEM((2 * hb, nc, span, 3 * REP_LANES), F32),
            pltpu.VMEM((2 * hb, nc, 8, span), F32),
        ],
        compiler_params=pltpu.CompilerParams(
            dimension_semantics=("arbitrary", "arbitrary"), vmem_limit_bytes=VMEM_LIMIT),
        name="mlstm",
    )(big4, big4, big4, big4, gt5, gate_b, norm_g)


def _log2(n):
    return n.bit_length() - 1


def _deltanet_kernel(q_ref, k_ref, v_ref, z_ref, gt_ref, p_ref,
                     ng_ref, y_ref, pq_ref, nn_ref, gl_ref, acc_ref,
                     *, seq, span, hb):
    nsp = seq // span
    lc = DN_CHUNK
    nck = seq // lc
    per = span // lc
    sh = _log2(lc)
    spi = 4

    def fold(full):
        lane_grp = lax.broadcasted_iota(jnp.int32, (lc, span), 1) >> sh
        out = full[(per - 1) * lc:per * lc]
        for c in range(per - 2, -1, -1):
            out = jnp.where(lane_grp == c, full[c * lc:(c + 1) * lc], out)
        return out

    def widen(col):
        lane_grp = lax.broadcasted_iota(jnp.int32, (lc, span), 1) >> sh
        out = jnp.broadcast_to(col[(per - 1) * lc:per * lc], (lc, span))
        for c in range(per - 2, -1, -1):
            out = jnp.where(lane_grp == c, jnp.broadcast_to(col[c * lc:(c + 1) * lc], (lc, span)), out)
        return out

    def blockdiag(wide_bf):
        ri = lax.broadcasted_iota(jnp.int32, (span, span), 0)
        ci = lax.broadcasted_iota(jnp.int32, (span, span), 1)
        tiled = jnp.concatenate([wide_bf] * per, axis=0)
        return jnp.where((ri >> sh) == (ci >> sh), tiled, jnp.zeros_like(tiled))

    def wide_inverse(aws, ri, cj):
        eye = (ri == cj).astype(F32)
        same16 = (ri >> 4) == (cj >> 4)
        ns = [jnp.where(same16, aw, 0.0).astype(BF16) for aw in aws]
        xs = [eye - n.astype(F32) for n in ns]
        ns = [_dot(n, blockdiag(n)).astype(BF16) for n in ns]
        for _ in range(2):
            rs = [_dot(jnp.concatenate([x.astype(BF16), n], axis=0), blockdiag(n)) for x, n in zip(xs, ns)]
            xs = [x + r[:lc] for x, r in zip(xs, rs)]
            ns = [r[lc:].astype(BF16) for r in rs]
        xs = [x + _dot(x.astype(BF16), blockdiag(n)) for x, n in zip(xs, ns)]
        lvl = 4
        while (1 << lvl) < lc:
            inner = (ri >> lvl) == (cj >> lvl)
            outer = (ri >> (lvl + 1)) == (cj >> (lvl + 1))
            pick = jnp.logical_and(outer, jnp.logical_not(inner))
            xb = [x.astype(BF16) for x in xs]
            xo = [_dot(x, blockdiag(jnp.where(pick, aw, 0.0).astype(BF16))).astype(BF16)
                  for x, aw in zip(xb, aws)]
            xs = [x - _dot(t, blockdiag(b)) for x, t, b in zip(xs, xo, xb)]
            lvl += 1
        return xs

    def prep(it, carry):
        ri_f = lax.broadcasted_iota(jnp.int32, (span, span), 0)
        ci_f = lax.broadcasted_iota(jnp.int32, (span, span), 1)
        same_f = (ri_f >> sh) == (ci_f >> sh)
        eye_bf = (ri_f == ci_f).astype(BF16)
        cum_bf = jnp.concatenate([jnp.logical_and(same_f, ri_f <= ci_f).astype(BF16),
                                  jnp.logical_and(same_f, ri_f >= ci_f).astype(BF16)], axis=1)
        ri = lax.broadcasted_iota(jnp.int32, (lc, span), 0)
        cj = lax.broadcasted_iota(jnp.int32, (lc, span), 1) & (lc - 1)
        zero6 = jnp.zeros((6, span), F32)
        zero2 = jnp.zeros((2, span), F32)

        units = [(hh, it * spi + j) for j in range(spi) for hh in range(hb)]
        rows_u = [pl.ds(pl.multiple_of(s * span, span), span) for _, s in units]
        hcols = [slice(hh * DN_DK, (hh + 1) * DN_DK) for hh, _ in units]
        k_bf = [k_ref[r, cs] for r, cs in zip(rows_u, hcols)]
        q_bf = [q_ref[r, cs] for r, cs in zip(rows_u, hcols)]
        kf = [t.astype(F32) for t in k_bf]
        qf = [t.astype(F32) for t in q_bf]
        vf = [v_ref[r, cs].astype(F32) for r, cs in zip(rows_u, hcols)]
        kq = [_dot_nt(jnp.concatenate([k, q], axis=0), k) for k, q in zip(k_bf, q_bf)]
        kkw = [fold(t[:span]) for t in kq]
        qkw = [fold(t[span:]) for t in kq]

        g8 = [gt_ref[s, hh] for hh, s in units]
        pr = [p_ref[hh] for hh, _ in units]
        beta_f = [_sigmoid(g[0:1]) for g in g8]
        beta_b = [_sigmoid(g[2:3]) for g in g8]
        g_f = [-jnp.exp(p[0:1]) * _softplus(g[1:2] + p[2:3]) for g, p in zip(g8, pr)]
        g_b = [-jnp.exp(p[1:2]) * _softplus(g[3:4] + p[3:4]) for g, p in zip(g8, pr)]
        g2 = [jnp.concatenate([a, b, zero6], axis=0) for a, b in zip(g_f, g_b)]
        cum = [_rows_times_mask(g, cum_bf) for g in g2]
        cum_fwd = [c[:, :span] for c in cum]
        cum_rev = [c[:, span:] for c in cum]
        tot = [a + b - g for a, b, g in zip(cum_fwd, cum_rev, g2)]
        cols = [_rows_to_cols(jnp.concatenate([bf, cf[0:1], t[0:1], bb, cr[1:2], t[1:2], zero2], axis=0),
                              eye_bf)
                for bf, cf, t, bb, cr in zip(beta_f, cum_fwd, tot, beta_b, cum_rev)]

        chains = [(u, fwd) for u in range(len(units)) for fwd in (True, False)]
        col = lambda u, fwd, k: cols[u][:, (0 if fwd else 3) + k:(0 if fwd else 3) + k + 1]
        beta_col = [col(u, f, 0) for u, f in chains]
        gc_col = [col(u, f, 1) for u, f in chains]
        tot_col = [col(u, f, 2) for u, f in chains]
        gc_row = [cum_fwd[u][0:1] if f else cum_rev[u][1:2] for u, f in chains]
        tot_row = [tot[u][0:1] if f else tot[u][1:2] for u, f in chains]
        decay = [jnp.where((cj <= ri) if f else (cj >= ri), jnp.exp(widen(gc) - gr), 0.0)
                 for (u, f), gc, gr in zip(chains, gc_col, gc_row)]
        attn = [qkw[u] * d for (u, f), d in zip(chains, decay)]
        aws = [jnp.where((cj < ri) if f else (cj > ri), kkw[u] * d * widen(bc), 0.0)
               for (u, f), d, bc in zip(chains, decay, beta_col)]
        tws = wide_inverse(aws, ri, cj)
        e_gc = [jnp.exp(gc) for gc in gc_col]
        rhs = [jnp.concatenate([(kf[u] * (bc * e)).astype(BF16), (vf[u] * bc).astype(BF16)], axis=1)
               for (u, f), bc, e in zip(chains, beta_col, e_gc)]
        wu_bf = [_dot(blockdiag(t.astype(BF16)), r).astype(BF16) for t, r in zip(tws, rhs)]
        awu = [_dot(blockdiag(a.astype(BF16)), w) for a, w in zip(attn, wu_bf)]
        qp = [(qf[u] * e - t[:, :DN_DK]).astype(BF16) for (u, f), e, t in zip(chains, e_gc, awu)]
        kd = [(kf[u] * jnp.exp(tc - gc)).astype(BF16) for (u, f), tc, gc in zip(chains, tot_col, gc_col)]
        gl_row = [jnp.exp(t) for t in tot_row]
        for n, (u, f) in enumerate(chains):
            hh, s = units[u]
            idx = 2 * hh + (0 if f else 1)
            for c in range(per):
                cr = slice(c * lc, (c + 1) * lc)
                pn = _dot_tn(kd[n][cr], wu_bf[n][cr])
                chunk = s * per + c
                pq_ref[idx, chunk, 0:DN_DK, :] = pn[:, :DN_DK].astype(BF16)
                pq_ref[idx, chunk, DN_DK:DN_DK + lc, :] = qp[n][cr]
                nn_ref[idx, chunk] = pn[:, DN_DK:]
                gl_ref[idx, chunk] = jnp.broadcast_to(gl_row[n][:, c * lc:c * lc + 1], (1, DN_DK))
        for u, (hh, s) in enumerate(units):
            acc_ref[hh, rows_u[u], :] = awu[2 * u][:, DN_DK:] + awu[2 * u + 1][:, DN_DK:]
        return carry

    lax.fori_loop(0, nsp // spi, prep, 0)

    def step(c, st, hh, idx):
        r = _dot(pq_ref[idx, c], st.astype(BF16))
        rows = pl.ds(pl.multiple_of(c * lc, lc), lc)
        acc_ref[hh, rows, :] += r[DN_DK:]
        return gl_ref[idx, c] * st - r[:DN_DK] + nn_ref[idx, c]

    def body(c, carry):
        out = []
        for hh in range(hb):
            out.append(step(c, carry[2 * hh], hh, 2 * hh))
            out.append(step(nck - 1 - c, carry[2 * hh + 1], hh, 2 * hh + 1))
        return tuple(out)

    s0 = jnp.zeros((DN_DK, DN_DK), F32)
    lax.fori_loop(0, nck, body, (s0,) * (2 * hb))

    def finish(t, carry):
        rows = pl.ds(pl.multiple_of(t * span, span), span)
        for hh in range(hb):
            cols_h = slice(hh * DN_DK, (hh + 1) * DN_DK)
            o = acc_ref[hh, rows, :]
            y = o * lax.rsqrt(jnp.mean(o * o, axis=-1, keepdims=True) + EPS) * ng_ref[...]
            z = z_ref[rows, cols_h].astype(F32)
            y_ref[rows, cols_h] = (y * (z * _sigmoid(z))).astype(y_ref.dtype)
        return carry

    lax.fori_loop(0, nsp, finish, 0)


def _deltanet(big4, gt5, params, norm_g, *, span, hb):
    _, b, s, _ = big4.shape
    nsp = s // span
    nck = s // DN_CHUNK
    wb = hb * DN_DK
    groups = DN_HEADS // hb
    slot0 = M_HEADS // hb
    q_spec = lambda t: pl.BlockSpec((None, None, s, wb), lambda i, h, t=t: (t, i, 0, h))
    return pl.pallas_call(
        functools.partial(_deltanet_kernel, seq=s, span=span, hb=hb),
        grid=(b, groups),
        in_specs=[
            q_spec(4), q_spec(5), q_spec(6), q_spec(7),
            pl.BlockSpec((None, nsp, hb, 8, span), lambda i, h: (i, 0, slot0 + h, 0, 0)),
            pl.BlockSpec((hb, 8, span), lambda i, h: (h, 0, 0)),
            pl.BlockSpec((1, DN_DK), lambda i, h: (0, 0)),
        ],
        out_specs=pl.BlockSpec((None, s, wb), lambda i, h: (i, 0, h)),
        out_shape=jax.ShapeDtypeStruct((b, s, D_MODEL), BF16),
        scratch_shapes=[
            pltpu.VMEM((2 * hb, nck, DN_DK + DN_CHUNK, DN_DK), BF16),
            pltpu.VMEM((2 * hb, nck, DN_DK, DN_DK), F32),
            pltpu.VMEM((2 * hb, nck, 1, DN_DK), F32),
            pltpu.VMEM((hb, s, DN_DK), F32),
        ],
        compiler_params=pltpu.CompilerParams(
            dimension_semantics=("arbitrary", "arbitrary"), vmem_limit_bytes=VMEM_LIMIT),
        name="deltanet",
    )(big4, big4, big4, big4, gt5, params, norm_g)


HALO = 16


def _merge_kernel(x_ref, ym_ref, yd_ref, sb_ref, sc_ref, sx_ref, cp_ref, xp_ref, cn_ref, xn_ref,
                  cw_ref, p0_ref, p1_ref, p2_ref, wb_ref, wo_ref, o_ref, *, tiles_per_seq):
    i = pl.program_id(0)
    tm = x_ref.shape[0]
    cx = sc_ref[...].astype(F32) * sx_ref[...].astype(F32)
    has_prev = jnp.where(i % tiles_per_seq != 0, 1.0, 0.0)
    has_next = jnp.where(i % tiles_per_seq != tiles_per_seq - 1, 1.0, 0.0)
    prev_row = cp_ref[HALO - 1:HALO, :].astype(F32) * xp_ref[HALO - 1:HALO, :].astype(F32) * has_prev
    next_row = cn_ref[0:1, :].astype(F32) * xn_ref[0:1, :].astype(F32) * has_next
    rid = lax.broadcasted_iota(jnp.int32, (8, D_MODEL), 0)
    cp = pltpu.roll(cx, 1, 0)
    cp = jnp.concatenate([jnp.where(rid == 0, prev_row, cp[0:8]), cp[8:]], axis=0)
    cn = pltpu.roll(cx, tm - 1, 0)
    cn = jnp.concatenate([cn[:tm - 8], jnp.where(rid == 7, next_row, cn[tm - 8:])], axis=0)
    conv = cw_ref[0:1, :] * cp + cw_ref[1:2, :] * cx + cw_ref[2:3, :] * cn
    yc = (sb_ref[...].astype(F32) * conv).astype(BF16)

    mixed = _sigmoid(p0_ref[...].astype(F32)) * _dot(ym_ref[...], wb_ref[0])
    mixed += _sigmoid(p1_ref[...].astype(F32)) * _dot(yd_ref[...], wb_ref[1])
    mixed += _sigmoid(p2_ref[...].astype(F32)) * _dot(yc, wb_ref[2])
    o_ref[...] = x_ref[...] + _dot(mixed.astype(BF16), wo_ref[...])


def _merge(x2, ym, yd, big2, conv_w, w_branch, w_out, *, layer, seq, tm):
    m = x2.shape[0]
    hpt = tm // HALO
    last = m // HALO - 1
    row = lambda: pl.BlockSpec((tm, D_MODEL), lambda i: (i, 0))
    col = lambda c: pl.BlockSpec((None, tm, D_MODEL), lambda i, c=c: (c, i, 0))
    prev = lambda c: pl.BlockSpec((None, HALO, D_MODEL),
                                  lambda i, c=c: (c, jnp.maximum(i * hpt - 1, 0), 0))
    nxt = lambda c: pl.BlockSpec((None, HALO, D_MODEL),
                                 lambda i, c=c: (c, jnp.minimum((i + 1) * hpt, last), 0))
    sc_b, sc_c, sc_x, pre0 = 8, 9, 10, 11
    return pl.pallas_call(
        functools.partial(_merge_kernel, tiles_per_seq=seq // tm),
        grid=(m // tm,),
        in_specs=[row(), row(), row(), col(sc_b), col(sc_c), col(sc_x),
                  prev(sc_c), prev(sc_x), nxt(sc_c), nxt(sc_x),
                  pl.BlockSpec((8, D_MODEL), lambda i: (0, 0)),
                  col(pre0), col(pre0 + 1), col(pre0 + 2),
                  pl.BlockSpec((None, 3, D_MODEL, D_MODEL), lambda i: (layer, 0, 0, 0)),
                  pl.BlockSpec((None, D_MODEL, D_MODEL), lambda i: (layer, 0, 0))],
        out_specs=row(),
        out_shape=jax.ShapeDtypeStruct((m, D_MODEL), F32),
        compiler_params=pltpu.CompilerParams(
            dimension_semantics=("arbitrary",), vmem_limit_bytes=VMEM_LIMIT),
        name="merge",
    )(x2, ym, yd, big2, big2, big2, big2, big2, big2, big2, conv_w,
      big2, big2, big2, w_branch, w_out)


def _mlp_kernel(x_ref, g_ref, wu_ref, wd_ref, gf_ref, o_ref, *, ff_chunk, final_norm):
    x = x_ref[...]
    h = (x * lax.rsqrt(jnp.mean(x * x, axis=-1, keepdims=True) + EPS) * g_ref[...]).astype(BF16)
    out = x
    for c in range(D_FF // ff_chunk):
        cols = slice(c * ff_chunk, (c + 1) * ff_chunk)
        up = jnp.maximum(_dot(h, wu_ref[:, cols]), 0.0)
        out = out + _dot((up * up).astype(BF16), wd_ref[cols, :])
    if final_norm:
        out = out * lax.rsqrt(jnp.mean(out * out, axis=-1, keepdims=True) + EPS) * gf_ref[...]
    o_ref[...] = out


def _mlp(x2, g, w_up, w_down, g_final, *, layer, tm, final_norm):
    m = x2.shape[0]
    return pl.pallas_call(
        functools.partial(_mlp_kernel, ff_chunk=1024, final_norm=final_norm),
        grid=(m // tm,),
        in_specs=[
            pl.BlockSpec((tm, D_MODEL), lambda i: (i, 0)),
            pl.BlockSpec((1, D_MODEL), lambda i: (0, 0)),
            pl.BlockSpec((None, D_MODEL, D_FF), lambda i: (layer, 0, 0)),
            pl.BlockSpec((None, D_FF, D_MODEL), lambda i: (layer, 0, 0)),
            pl.BlockSpec((1, D_MODEL), lambda i: (0, 0)),
        ],
        out_specs=pl.BlockSpec((tm, D_MODEL), lambda i: (i, 0)),
        out_shape=jax.ShapeDtypeStruct((m, D_MODEL), F32),
        compiler_params=pltpu.CompilerParams(
            dimension_semantics=("arbitrary",), vmem_limit_bytes=VMEM_LIMIT),
        name="mlp",
    )(x2, g, w_up, w_down, g_final)


def _prep_w_in(w_in):
    d = w_in.shape[0]
    c_mg = 4 * D_MODEL
    c_dn = c_mg + 4 * M_HEADS
    c_dg = c_dn + 4 * D_MODEL
    c_sc = c_dg + 4 * DN_HEADS
    w_big = jnp.concatenate([w_in[:, :c_mg], w_in[:, c_dn:c_dg], w_in[:, c_sc:]], axis=1)
    mg = w_in[:, c_mg:c_dn].reshape(d, 4, M_HEADS).transpose(2, 1, 0)
    dg = w_in[:, c_dg:c_sc].reshape(d, 4, DN_HEADS).transpose(2, 1, 0)
    pad = lambda t: jnp.pad(t, ((0, 0), (0, 4), (0, 0))).reshape(-1, d)
    wg_t = jnp.concatenate([pad(mg), pad(dg)], axis=0)
    wg_t = jnp.pad(wg_t, ((0, GATE_ROWS - wg_t.shape[0]), (0, 0)))
    w_tiles = w_big.astype(BF16).reshape(d, N_BIG // D_MODEL, D_MODEL).transpose(1, 0, 2)
    return w_tiles, wg_t.astype(BF16)


def _rows8(vals, width):
    h, r = vals.shape
    t = jnp.pad(vals.astype(F32), ((0, 0), (0, 8 - r)))
    return jnp.broadcast_to(t[:, :, None], (h, 8, width))


def _layer(x2, batch, seq, layer, p, big_w, g_final, final_norm):
    m = x2.shape[0]
    span = SPAN
    hb = HEADS_PER_STEP
    dn_conv = jnp.pad(p["dn_conv_w"].astype(F32), ((0, 5), (0, 0)))
    big, gt = _in_proj(x2, p["norm_mix_g"].reshape(1, -1), big_w["w_big"], big_w["wg_t"], dn_conv,
                       layer=layer, seq=seq, span=span)
    big3 = big.reshape(N_BIG // D_MODEL, batch, seq, D_MODEL)
    gt5 = gt.reshape(batch, seq // span, GATE_ROWS // 8, 8, span)

    gate_b = _rows8(p["m_gate_b"].T, span)
    ym = _mlstm(big3, gt5, gate_b, p["m_norm_g"].reshape(M_HEADS, 1, M_DK), span=span, hb=hb)

    dn_params = _rows8(jnp.concatenate([p["dn_a_log"].T, p["dn_dt_bias"].T], axis=1), span)
    yd = _deltanet(big3, gt5, dn_params, p["dn_norm_g"].reshape(1, -1), span=span, hb=hb)

    sc_w = jnp.pad(p["sc_conv_w"].astype(F32), ((0, 5), (0, 0)))
    x2 = _merge(x2, ym.reshape(m, -1), yd.reshape(m, -1), big, sc_w,
                big_w["w_branch"], big_w["w_out"], layer=layer, seq=seq, tm=512)
    x2 = _mlp(x2, p["norm_mlp_g"].reshape(1, -1), big_w["w_up"], big_w["w_down"],
              g_final.reshape(1, -1), layer=layer, tm=512, final_norm=final_norm)
    return x2


def kernel(x, norm_mix_g, w_in, m_gate_b, m_norm_g, dn_conv_w, dn_a_log, dn_dt_bias,
           dn_norm_g, sc_conv_w, w_branch, w_out, norm_mlp_g, w_up, w_down, norm_final_g):
    batch, seq, d = x.shape
    depth = w_in.shape[0]
    x2 = x.reshape(batch * seq, d)
    w_big, wg_t = jax.vmap(_prep_w_in)(w_in)
    big_w = dict(w_big=w_big, wg_t=wg_t, w_branch=w_branch.astype(BF16), w_out=w_out.astype(BF16),
                 w_up=w_up.astype(BF16), w_down=w_down.astype(BF16))
    small = dict(norm_mix_g=norm_mix_g, m_gate_b=m_gate_b, m_norm_g=m_norm_g,
                 dn_conv_w=dn_conv_w, dn_a_log=dn_a_log, dn_dt_bias=dn_dt_bias,
                 dn_norm_g=dn_norm_g, sc_conv_w=sc_conv_w, norm_mlp_g=norm_mlp_g)
    for l in range(depth):
        p = {k: v[l] for k, v in small.items()}
        x2 = _layer(x2, batch, seq, l, p, big_w, norm_final_g, final_norm=(l == depth - 1))
    return x2.reshape(batch, seq, d)
```
